```python
import math
import jax, jax.numpy as jnp
from jax import lax
import numpy as np

D_MODEL = 2048
BATCH = 4
SEQ = 4096
DEPTH = 4

N_MEM = 256
A_HEADS = 8
A_DQK = 64
A_DV = 128
A_WIDTH = A_HEADS * A_DV
B_HEADS = 8
B_DK = 128
B_DV = 128
B_WIDTH = B_HEADS * B_DV
HGRN_CHUNK = 64
C_HEADS = 4
C_DH = 256
C_WIDTH = C_HEADS * C_DH
N_BRANCH = 3
BRANCH_WIDTH = 1024
Q_BLOCK = 128
EPS = 1e-6
IN_SIZES = (
    2 * A_HEADS * A_DQK,
    2 * A_HEADS * A_DQK,
    A_WIDTH,
    A_WIDTH,
    B_HEADS * B_DK,
    B_WIDTH,
    B_HEADS * B_DK,
    B_WIDTH,
    C_WIDTH,
    C_WIDTH,
    N_BRANCH * D_MODEL,
)
IN_WIDTH = 16384

kernel_name = "hybrid_diffattn_hgrn2_memxattn_gated_merge"


def rms_norm(x, g):
    xf = x.astype(jnp.float32)
    y = xf * lax.rsqrt(jnp.mean(xf * xf, axis=-1, keepdims=True) + EPS)
    return (y * g.astype(jnp.float32)).astype(x.dtype)


def split_columns(p):
    parts, off = [], 0
    for n in IN_SIZES:
        parts.append(p[..., off:off + n])
        off += n
    return parts


def diff_attention(qa, ka, va, lam_p, subln_g, lam_init):
    B, S, _ = qa.shape
    q = qa.reshape(B, S, A_HEADS, 2, A_DQK).transpose(0, 2, 3, 1, 4) * (A_DQK ** -0.5)
    k = ka.reshape(B, S, A_HEADS, 2, A_DQK).transpose(0, 2, 3, 1, 4)
    v = va.reshape(B, S, A_HEADS, A_DV).transpose(0, 2, 1, 3)
    lp = lam_p.astype(jnp.float32)
    lam = jnp.exp(jnp.sum(lp[0] * lp[1])) - jnp.exp(jnp.sum(lp[2] * lp[3])) + lam_init
    outs = []
    for i in range(S // Q_BLOCK):
        lo, hi = i * Q_BLOCK, (i + 1) * Q_BLOCK
        s = jnp.einsum('bhmqd,bhmkd->bhmqk', q[:, :, :, lo:hi].astype(jnp.float32),
                       k[:, :, :, :hi].astype(jnp.float32))
        mask = jnp.arange(hi)[None, :] <= (lo + jnp.arange(Q_BLOCK))[:, None]
        pr = jax.nn.softmax(jnp.where(mask, s, -jnp.inf), axis=-1)
        attn = pr[:, :, 0] - lam * pr[:, :, 1]
        outs.append(jnp.einsum('bhqk,bhkd->bhqd', attn.astype(v.dtype), v[:, :, :hi]))
    o = jnp.concatenate(outs, axis=2)
    o = rms_norm(o, subln_g) * (1.0 - lam_init)
    return o.transpose(0, 2, 1, 3).reshape(B, S, A_WIDTH)


def hgrn2_chunked(q, k, v, log_f):
    B, S, H, dk = q.shape
    dv = v.shape[-1]
    nc = S // HGRN_CHUNK

    def to_chunks(t):
        return t.reshape(B, nc, HGRN_CHUNK, H, t.shape[-1]).transpose(1, 0, 3, 2, 4)

    causal = jnp.tril(jnp.ones((HGRN_CHUNK, HGRN_CHUNK), dtype=bool))

    def step(state, inp):
        qc, kc, vc, gc = inp
        b = jnp.cumsum(gc, axis=2)
        o_inter = jnp.einsum('bhtd,bhde->bhte', qc * jnp.exp(b), state)
        diff = b[:, :, :, None, :] - b[:, :, None, :, :]
        decay = jnp.exp(jnp.where(causal[:, :, None], diff, -jnp.inf))
        a = jnp.einsum('bhtd,bhsd,bhtsd->bhts', qc, kc, decay)
        o = o_inter + jnp.einsum('bhts,bhse->bhte', a, vc)
        b_last = b[:, :, -1:, :]
        k_dec = kc * jnp.exp(b_last - b)
        state = jnp.exp(b_last[:, :, 0, :])[..., None] * state + jnp.einsum('bhsd,bhse->bhde', k_dec, vc)
        return state, o

    s0 = jnp.zeros((B, H, dk, dv), jnp.float32)
    _, o = lax.scan(step, s0, (to_chunks(q), to_chunks(k), to_chunks(v), to_chunks(log_f)))
    return o.transpose(1, 0, 3, 2, 4).reshape(B, S, H, dv)


def hgrn2_branch(fb, ib, qb, lb, hgrn_g):
    B, S, _ = fb.shape
    lbl = lb.reshape(B_HEADS, B_DK)
    z = fb.reshape(B, S, B_HEADS, B_DK).astype(jnp.float32)
    log_f = jnp.logaddexp(jnp.log(lbl), jnp.log1p(-lbl) + jax.nn.log_sigmoid(z))
    k = (1.0 - lbl) * jax.nn.sigmoid(-z)
    q = jax.nn.silu(qb.reshape(B, S, B_HEADS, B_DK).astype(jnp.float32))
    v = ib.reshape(B, S, B_HEADS, B_DV).astype(jnp.float32)
    o = hgrn2_chunked(q, k, v, log_f)
    o = rms_norm(o, hgrn_g)
    return o.reshape(B, S, B_WIDTH).astype(fb.dtype)


def memory_cross_attention(qc, mem_n, w_kv):
    B, S, _ = qc.shape
    kv = mem_n @ w_kv
    km = kv[..., :C_WIDTH].reshape(B, N_MEM, C_HEADS, C_DH)
    vm = kv[..., C_WIDTH:].reshape(B, N_MEM, C_HEADS, C_DH)
    qh = qc.reshape(B, S, C_HEADS, C_DH)
    s = jnp.einsum('bqhd,bkhd->bhqk', qh.astype(jnp.float32), km.astype(jnp.float32)) * (C_DH ** -0.5)
    p = jax.nn.softmax(s, axis=-1).astype(vm.dtype)
    return jnp.einsum('bhqk,bkhd->bqhd', p, vm).reshape(B, S, C_WIDTH)


def setup_inputs(seed: int = 0) -> dict:
    key = jax.random.key(seed)
    ks = jax.random.split(key, 14)
    f32 = jnp.float32
    return {
        "x": jax.random.normal(ks[0], (BATCH, SEQ, D_MODEL), f32),
        "mem": jax.random.normal(ks[1], (BATCH, N_MEM, D_MODEL), f32),
        "norm_g": 1.0 + 0.02 * jax.random.normal(ks[2], (DEPTH, D_MODEL), f32),
        "w_in": jax.random.normal(ks[3], (DEPTH, D_MODEL, IN_WIDTH), f32) * D_MODEL ** -0.5,
        "diff_lambda": 0.1 * jax.random.normal(ks[4], (DEPTH, 4, A_DQK), f32),
        "diff_subln_g": 1.0 + 0.02 * jax.random.normal(ks[5], (DEPTH, A_DV), f32),
        "hgrn_lb_raw": 0.1 * jax.random.normal(ks[6], (DEPTH, B_HEADS * B_DK), f32),
        "hgrn_norm_g": 1.0 + 0.02 * jax.random.normal(ks[7], (DEPTH, B_DV), f32),
        "mem_norm_g": 1.0 + 0.02 * jax.random.normal(ks[8], (DEPTH, D_MODEL), f32),
        "w_kv_mem": jax.random.normal(ks[9], (DEPTH, D_MODEL, 2 * C_WIDTH), f32) * D_MODEL ** -0.5,
        "w_branch": jax.random.normal(ks[10], (DEPTH, N_BRANCH, BRANCH_WIDTH, D_MODEL), f32) * BRANCH_WIDTH ** -0.5,
        "w_out": jax.random.normal(ks[11], (DEPTH, D_MODEL, D_MODEL), f32) * D_MODEL ** -0.5,
        "final_norm_g": 1.0 + 0.02 * jax.random.normal(ks[12], (D_MODEL,), f32),
    }


def reference(x, mem, norm_g, w_in, diff_lambda, diff_subln_g, hgrn_lb_raw, hgrn_norm_g,
              mem_norm_g, w_kv_mem, w_branch, w_out, final_norm_g):
    lb_all = jnp.cumsum(jax.nn.softmax(hgrn_lb_raw.astype(jnp.float32), axis=0), axis=0)
    lb_all = lb_all - lb_all[0:1]
    for l in range(DEPTH):
        lam_init = 0.8 - 0.6 * math.exp(-0.3 * l)
        h = rms_norm(x, norm_g[l])
        p = h @ w_in[l]
        qa, ka, va, ga, fb, ib, qb, gb, qc, gc, gl = split_columns(p)
        oa = diff_attention(qa, ka, va, diff_lambda[l], diff_subln_g[l], lam_init) * jax.nn.silu(ga)
        ob = hgrn2_branch(fb, ib, qb, lb_all[l], hgrn_norm_g[l]) * jax.nn.silu(gb)
        mem_n = rms_norm(mem, mem_norm_g[l])
        oc = memory_cross_attention(qc, mem_n, w_kv_mem[l]) * jax.nn.silu(gc)
        gates = jax.nn.sigmoid(gl.reshape(gl.shape[0], gl.shape[1], N_BRANCH, D_MODEL))
        y = (gates[:, :, 0] * (oa @ w_branch[l, 0])
             + gates[:, :, 1] * (ob @ w_branch[l, 1])
             + gates[:, :, 2] * (oc @ w_branch[l, 2]))
        x = x + y @ w_out[l]
    return rms_norm(x, final_norm_g)
```

```python
import functools
import math

import jax
import jax.numpy as jnp
from jax import lax
from jax.experimental import pallas as pl
from jax.experimental.pallas import tpu as pltpu

F32 = jnp.float32
BF16 = jnp.bfloat16

D_MODEL = 2048
N_MEM = 256
A_HEADS = 8
A_DQK = 64
A_DV = 128
B_HEADS = 8
B_DK = 128
C_HEADS = 4
C_DH = 256
BRANCH_WIDTH = 1024
N_BRANCH = 3
EPS = 1e-6

COL_QA, COL_KA, COL_VA, COL_GA = 0, 8, 16, 24
COL_FB, COL_IB, COL_QB, COL_GB = 32, 40, 48, 56
COL_QC_1024, COL_GC_1024 = 8, 9
COL_GL_2048 = 5

VMEM_LIMIT_BYTES = 56 * 1024 * 1024

SUB = 16


def _nt_dot(a, b):
    return lax.dot_general(a, b, (((1,), (1,)), ((), ())), preferred_element_type=F32)


def _tn_dot(a, b):
    return lax.dot_general(a, b, (((0,), (0,)), ((), ())), preferred_element_type=F32)


def _norm_proj_kernel(x_ref, g_ref, w_ref, o_ref, h_ref):
    @pl.when(pl.program_id(1) == 0)
    def _():
        x = x_ref[...]
        ms = jnp.mean(x * x, axis=-1, keepdims=True)
        h_ref[...] = (x * lax.rsqrt(ms + EPS) * g_ref[...]).astype(BF16)

    o_ref[...] = jnp.dot(h_ref[...], w_ref[...], preferred_element_type=F32).astype(o_ref.dtype)


def _norm_proj(x, g, w, *, tm, tn):
    m, k = x.shape
    n = w.shape[1]
    return pl.pallas_call(
        _norm_proj_kernel,
        grid=(m // tm, n // tn),
        in_specs=[
            pl.BlockSpec((tm, k), lambda i, j: (i, 0)),
            pl.BlockSpec((1, k), lambda i, j: (0, 0)),
            pl.BlockSpec((k, tn), lambda i, j: (0, j)),
        ],
        out_specs=pl.BlockSpec((tm, tn), lambda i, j: (i, j)),
        out_shape=jax.ShapeDtypeStruct((m, n), BF16),
        scratch_shapes=[pltpu.VMEM((tm, k), BF16)],
        compiler_params=pltpu.CompilerParams(
            dimension_semantics=("arbitrary", "arbitrary"),
            vmem_limit_bytes=VMEM_LIMIT_BYTES),
        name="norm_proj",
    )(x, g.reshape(1, k), w)


def _diff_attn_kernel(q_ref, k_ref, v_ref, gate_ref, lam_ref, g_ref, o_ref,
                      m_ref, l_ref, acc_ref, *, tq, lam_init):
    i = pl.program_id(2)
    q = q_ref[...] * jnp.asarray(A_DQK ** -0.5, BF16)
    lane = lax.broadcasted_iota(jnp.int32, q.shape, 1)
    zero = jnp.zeros_like(q)
    qs = (jnp.where(lane < A_DQK, q, zero), jnp.where(lane >= A_DQK, q, zero))

    m_ref[...] = jnp.full(m_ref.shape, -jnp.inf, F32)
    l_ref[...] = jnp.zeros(l_ref.shape, F32)
    acc_ref[...] = jnp.zeros(acc_ref.shape, F32)

    def update(j, masked):
        r0 = pl.multiple_of(j * tq, tq)
        kb = k_ref[pl.ds(r0, tq), :]
        vb = v_ref[pl.ds(r0, tq), :]
        if masked:
            row = lax.broadcasted_iota(jnp.int32, (tq, tq), 0)
            col = lax.broadcasted_iota(jnp.int32, (tq, tq), 1)
            keep = col <= row
        for mp in range(2):
            s = _nt_dot(qs[mp], kb)
            if masked:
                s = jnp.where(keep, s, -jnp.inf)
            m_old = m_ref[mp]
            m_new = jnp.maximum(m_old, jnp.max(s, axis=-1, keepdims=True))
            alpha = jnp.exp(m_old - m_new)
            p = jnp.exp(s - m_new)
            l_ref[mp] = alpha * l_ref[mp] + jnp.sum(p, axis=-1, keepdims=True)
            acc_ref[mp] = alpha * acc_ref[mp] + jnp.dot(
                p.astype(BF16), vb, preferred_element_type=F32)
            m_ref[mp] = m_new

    def body(j, carry):
        update(j, False)
        return carry

    lax.fori_loop(0, i, body, 0)
    update(i, True)

    lp = lam_ref[...]
    lam = (jnp.exp(jnp.sum(lp[0:1] * lp[1:2], axis=-1, keepdims=True))
           - jnp.exp(jnp.sum(lp[2:3] * lp[3:4], axis=-1, keepdims=True)) + lam_init)
    o = acc_ref[0] / l_ref[0] - lam * (acc_ref[1] / l_ref[1])
    ms = jnp.mean(o * o, axis=-1, keepdims=True)
    o = o * lax.rsqrt(ms + EPS) * g_ref[...] * (1.0 - lam_init)
    o_ref[...] = (o * jax.nn.silu(gate_ref[...].astype(F32))).astype(o_ref.dtype)


def _diff_attn(p, lam_p, subln_g, *, batch, seq, lam_init, tq):
    t = batch * seq
    nq = seq // tq
    kernel = functools.partial(_diff_attn_kernel, tq=tq, lam_init=lam_init)
    return pl.pallas_call(
        kernel,
        grid=(batch, A_HEADS, nq),
        in_specs=[
            pl.BlockSpec((tq, 128), lambda b, h, i: (b * nq + i, COL_QA + h)),
            pl.BlockSpec((seq, 128), lambda b, h, i: (b, COL_KA + h)),
            pl.BlockSpec((seq, 128), lambda b, h, i: (b, COL_VA + h)),
            pl.BlockSpec((tq, 128), lambda b, h, i: (b * nq + i, COL_GA + h)),
            pl.BlockSpec((4, A_DQK), lambda b, h, i: (0, 0)),
            pl.BlockSpec((1, A_DV), lambda b, h, i: (0, 0)),
        ],
        out_specs=pl.BlockSpec((tq, 128), lambda b, h, i: (b * nq + i, h)),
        out_shape=jax.ShapeDtypeStruct((t, BRANCH_WIDTH), BF16),
        scratch_shapes=[
            pltpu.VMEM((2, tq, 1), F32),
            pltpu.VMEM((2, tq, 1), F32),
            pltpu.VMEM((2, tq, A_DV), F32),
        ],
        compiler_params=pltpu.CompilerParams(
            dimension_semantics=("arbitrary", "arbitrary", "arbitrary"),
            vmem_limit_bytes=VMEM_LIMIT_BYTES),
        name="diff_attn",
    )(p, p, p, p, lam_p, subln_g.reshape(1, A_DV))


def _hgrn_kernel(f_ref, i_ref, q_ref, gate_ref, lbraw_ref, ng_ref, o_ref,
                 st_ref, b_s, q_s, qe_s, k_s, o_s, *, layer, rows):
    @pl.when(pl.program_id(1) == 0)
    def _():
        st_ref[...] = jnp.zeros(st_ref.shape, F32)

    raw = lbraw_ref[...]
    e = jnp.exp(raw - jnp.max(raw, axis=0, keepdims=True))
    w = e / jnp.sum(e, axis=0, keepdims=True)
    lb = jnp.zeros((1, raw.shape[1]), F32)
    for j in range(1, layer + 1):
        lb = lb + w[j:j + 1]

    z = f_ref[...].astype(F32)
    logf = jnp.logaddexp(jnp.log(lb), jnp.log1p(-lb) + jax.nn.log_sigmoid(z))
    k_s[...] = (1.0 - lb) * jax.nn.sigmoid(-z)
    q = jax.nn.silu(q_ref[...].astype(F32))
    q_s[...] = q

    r = lax.broadcasted_iota(jnp.int32, (rows, rows), 0)
    c = lax.broadcasted_iota(jnp.int32, (rows, rows), 1)
    tri = jnp.where((r // SUB == c // SUB) & (c <= r), 1.0, 0.0).astype(BF16)
    hi = logf.astype(BF16)
    rem = logf - hi.astype(F32)
    mid = rem.astype(BF16)
    lo = (rem - mid.astype(F32)).astype(BF16)
    b = (jnp.dot(tri, hi, preferred_element_type=F32)
         + jnp.dot(tri, mid, preferred_element_type=F32)
         + jnp.dot(tri, lo, preferred_element_type=F32))
    b_s[...] = b
    qe_s[...] = q * jnp.exp(b)

    t_idx = lax.broadcasted_iota(jnp.int32, (SUB, 1), 0)

    def body(n, carry):
        r0 = pl.multiple_of(n * SUB, SUB)
        for h in range(B_HEADS):
            cs = slice(h * B_DK, (h + 1) * B_DK)
            bn = b_s[pl.ds(r0, SUB), cs]
            qn = q_s[pl.ds(r0, SUB), cs]
            kn = k_s[pl.ds(r0, SUB), cs]
            qe = qe_s[pl.ds(r0, SUB), cs]
            vn = i_ref[pl.ds(r0, SUB), cs]
            vf = vn.astype(F32)
            st = st_ref[h]
            acc = _nt_dot(qe.astype(BF16), st.astype(BF16))
            for s in range(SUB):
                arg = jnp.where(t_idx >= s, bn - bn[s:s + 1], -jnp.inf)
                wts = qn * (kn[s:s + 1] * jnp.exp(arg))
                acc = acc + jnp.sum(wts, axis=-1, keepdims=True) * vf[s:s + 1]
            o_s[pl.ds(r0, SUB), cs] = acc
            bl = bn[SUB - 1:SUB]
            kd = kn * jnp.exp(bl - bn)
            st_ref[h] = st * jnp.exp(bl) + _tn_dot(vn, kd.astype(BF16))
        return carry

    lax.fori_loop(0, rows // SUB, body, 0)

    for h in range(B_HEADS):
        cs = slice(h * B_DK, (h + 1) * B_DK)
        o = o_s[:, cs]
        ms = jnp.mean(o * o, axis=-1, keepdims=True)
        y = o * lax.rsqrt(ms + EPS) * ng_ref[...]
        o_ref[:, cs] = (y * jax.nn.silu(gate_ref[:, cs].astype(F32))).astype(o_ref.dtype)


def _hgrn(p, lb_raw, norm_g, *, batch, seq, layer, rows):
    t = batch * seq
    nr = seq // rows
    kernel = functools.partial(_hgrn_kernel, layer=layer, rows=rows)
    blk = lambda col: pl.BlockSpec((rows, 1024), lambda b, i: (b * nr + i, col // 8))
    depth = lb_raw.shape[0]
    return pl.pallas_call(
        kernel,
        grid=(batch, nr),
        in_specs=[
            blk(COL_FB), blk(COL_IB), blk(COL_QB), blk(COL_GB),
            pl.BlockSpec((depth, 1024), lambda b, i: (0, 0)),
            pl.BlockSpec((1, B_DK), lambda b, i: (0, 0)),
        ],
        out_specs=pl.BlockSpec((rows, 1024), lambda b, i: (b * nr + i, 0)),
        out_shape=jax.ShapeDtypeStruct((t, BRANCH_WIDTH), BF16),
        scratch_shapes=[
            pltpu.VMEM((B_HEADS, B_DK, B_DK), F32),
            pltpu.VMEM((rows, 1024), F32),
            pltpu.VMEM((rows, 1024), F32),
            pltpu.VMEM((rows, 1024), F32),
            pltpu.VMEM((rows, 1024), F32),
            pltpu.VMEM((rows, 1024), F32),
        ],
        compiler_params=pltpu.CompilerParams(
            dimension_semantics=("arbitrary", "arbitrary"),
            vmem_limit_bytes=VMEM_LIMIT_BYTES),
        name="hgrn2",
    )(p, p, p, p, lb_raw, norm_g.reshape(1, B_DK))


def _xattn_kernel(q_ref, gate_ref, k_ref, v_ref, o_ref):
    for h in range(C_HEADS):
        cs = slice(h * C_DH, (h + 1) * C_DH)
        s = _nt_dot(q_ref[:, cs], k_ref[:, cs]) * (C_DH ** -0.5)
        m = jnp.max(s, axis=-1, keepdims=True)
        pr = jnp.exp(s - m)
        l = jnp.sum(pr, axis=-1, keepdims=True)
        o = jnp.dot(pr.astype(BF16), v_ref[:, cs], preferred_element_type=F32) / l
        o_ref[:, cs] = (o * jax.nn.silu(gate_ref[:, cs].astype(F32))).astype(o_ref.dtype)


def _xattn(p, kv, *, batch, seq, tq):
    t = batch * seq
    nq = seq // tq
    return pl.pallas_call(
        _xattn_kernel,
        grid=(batch, nq),
        in_specs=[
            pl.BlockSpec((tq, 1024), lambda b, i: (b * nq + i, COL_QC_1024)),
            pl.BlockSpec((tq, 1024), lambda b, i: (b * nq + i, COL_GC_1024)),
            pl.BlockSpec((N_MEM, 1024), lambda b, i: (b, 0)),
            pl.BlockSpec((N_MEM, 1024), lambda b, i: (b, 1)),
        ],
        out_specs=pl.BlockSpec((tq, 1024), lambda b, i: (b * nq + i, 0)),
        out_shape=jax.ShapeDtypeStruct((t, BRANCH_WIDTH), BF16),
        compiler_params=pltpu.CompilerParams(
            dimension_semantics=("arbitrary", "arbitrary"),
            vmem_limit_bytes=VMEM_LIMIT_BYTES),
        name="mem_xattn",
    )(p, p, kv, kv)


def _merge_kernel(oa_ref, ob_ref, oc_ref, g0_ref, g1_ref, g2_ref, x_ref, wb_ref, wo_ref,
                  fg_ref, o_ref, *, final):
    y = None
    for o_r, g_r, j in ((oa_ref, g0_ref, 0), (ob_ref, g1_ref, 1), (oc_ref, g2_ref, 2)):
        term = jax.nn.sigmoid(g_r[...].astype(F32)) * jnp.dot(
            o_r[...], wb_ref[j], preferred_element_type=F32)
        y = term if y is None else y + term
    xn = x_ref[...] + jnp.dot(y.astype(BF16), wo_ref[...], preferred_element_type=F32)
    if final:
        ms = jnp.mean(xn * xn, axis=-1, keepdims=True)
        xn = xn * lax.rsqrt(ms + EPS) * fg_ref[...]
    o_ref[...] = xn


def _merge(oa, ob, oc, p, x, wb, wo, final_g, *, tm, final):
    t = x.shape[0]
    row = lambda i: (i, 0)
    const2 = lambda i: (0, 0)
    kernel = functools.partial(_merge_kernel, final=final)
    return pl.pallas_call(
        kernel,
        grid=(t // tm,),
        in_specs=[
            pl.BlockSpec((tm, BRANCH_WIDTH), row),
            pl.BlockSpec((tm, BRANCH_WIDTH), row),
            pl.BlockSpec((tm, BRANCH_WIDTH), row),
            pl.BlockSpec((tm, D_MODEL), lambda i: (i, COL_GL_2048)),
            pl.BlockSpec((tm, D_MODEL), lambda i: (i, COL_GL_2048 + 1)),
            pl.BlockSpec((tm, D_MODEL), lambda i: (i, COL_GL_2048 + 2)),
            pl.BlockSpec((tm, D_MODEL), row),
            pl.BlockSpec((N_BRANCH, BRANCH_WIDTH, D_MODEL), lambda i: (0, 0, 0),
                         pipeline_mode=pl.Buffered(1)),
            pl.BlockSpec((D_MODEL, D_MODEL), const2, pipeline_mode=pl.Buffered(1)),
            pl.BlockSpec((1, D_MODEL), const2),
        ],
        out_specs=pl.BlockSpec((tm, D_MODEL), row),
        out_shape=jax.ShapeDtypeStruct((t, D_MODEL), F32),
        compiler_params=pltpu.CompilerParams(
            dimension_semantics=("arbitrary",),
            vmem_limit_bytes=VMEM_LIMIT_BYTES),
        name="merge_out",
    )(oa, ob, oc, p, p, p, x, wb, wo, final_g.reshape(1, D_MODEL))


def kernel(x, mem, norm_g, w_in, diff_lambda, diff_subln_g, hgrn_lb_raw, hgrn_norm_g,
           mem_norm_g, w_kv_mem, w_branch, w_out, final_norm_g):
    batch, seq, d = x.shape
    depth = w_in.shape[0]
    xt = x.reshape(batch * seq, d)
    memt = mem.reshape(batch * N_MEM, d)
    w_in_b = w_in.astype(BF16)
    w_kv_b = w_kv_mem.astype(BF16)
    w_br_b = w_branch.astype(BF16)
    w_out_b = w_out.astype(BF16)
    lb_raw = hgrn_lb_raw.astype(F32)
    for l in range(depth):
        lam_init = 0.8 - 0.6 * math.exp(-0.3 * l)
        p = _norm_proj(xt, norm_g[l], w_in_b[l], tm=1024, tn=1024)
        kv = _norm_proj(memt, mem_norm_g[l], w_kv_b[l], tm=1024, tn=1024)
        oa = _diff_attn(p, diff_lambda[l], diff_subln_g[l], batch=batch, seq=seq,
                        lam_init=lam_init, tq=256)
        ob = _hgrn(p, lb_raw, hgrn_norm_g[l], batch=batch, seq=seq, layer=l, rows=256)
        oc = _xattn(p, kv, batch=batch, seq=seq, tq=512)
        xt = _merge(oa, ob, oc, p, xt, w_br_b[l], w_out_b[l], final_norm_g,
                    tm=256, final=(l == depth - 1))
    return xt.reshape(batch, seq, d)
```

```python
import functools
import math

import jax
import jax.numpy as jnp
from jax import lax
from jax.experimental import pallas as pl
from jax.experimental.pallas import tpu as pltpu

F32 = jnp.float32
BF16 = jnp.bfloat16

D_MODEL = 2048
N_MEM = 256
A_HEADS = 8
A_DQK = 64
A_DV = 128
B_HEADS = 8
B_DK = 128
C_HEADS = 4
C_DH = 256
BRANCH_WIDTH = 1024
N_BRANCH = 3
EPS = 1e-6

COL_QA, COL_KA, COL_VA, COL_GA = 0, 8, 16, 24
COL_FB, COL_IB, COL_QB, COL_GB = 32, 40, 48, 56
COL_QC_1024, COL_GC_1024 = 8, 9
COL_GL_2048 = 5

VMEM_LIMIT_BYTES = 56 * 1024 * 1024

BF16_SUBLANES = 16
SUB = 16


def _nt_dot(a, b):
    return lax.dot_general(a, b, (((1,), (1,)), ((), ())), preferred_element_type=F32)


def _tn_dot(a, b):
    return lax.dot_general(a, b, (((0,), (0,)), ((), ())), preferred_element_type=F32)


def _norm_proj_kernel(x_ref, g_ref, w_ref, o_ref, h_ref):
    @pl.when(pl.program_id(1) == 0)
    def _():
        x = x_ref[...]
        ms = jnp.mean(x * x, axis=-1, keepdims=True)
        h_ref[...] = (x * lax.rsqrt(ms + EPS) * g_ref[...]).astype(BF16)

    o_ref[...] = jnp.dot(h_ref[...], w_ref[...], preferred_element_type=F32).astype(o_ref.dtype)


def _norm_proj(x, g, w, *, layer, tm, tn):
    m, k = x.shape
    n = w.shape[2]
    return pl.pallas_call(
        _norm_proj_kernel,
        grid=(m // tm, n // tn),
        in_specs=[
            pl.BlockSpec((tm, k), lambda i, j: (i, 0)),
            pl.BlockSpec((1, k), lambda i, j: (0, 0)),
            pl.BlockSpec((None, k, tn), lambda i, j: (layer, 0, j)),
        ],
        out_specs=pl.BlockSpec((tm, tn), lambda i, j: (i, j)),
        out_shape=jax.ShapeDtypeStruct((m, n), BF16),
        scratch_shapes=[pltpu.VMEM((tm, k), BF16)],
        compiler_params=pltpu.CompilerParams(
            dimension_semantics=("arbitrary", "arbitrary"),
            vmem_limit_bytes=VMEM_LIMIT_BYTES),
        name="norm_proj",
    )(x, g.reshape(1, k), w)


def _diff_attn_kernel(q_ref, k_ref, v_ref, gate_ref, lam_ref, g_ref, o_ref,
                      vt_ref, s_ref, p_ref, acc_ref, *, tq, tk, seq, lam_init):
    i = pl.program_id(2)
    nd = tq // tk
    assert nd >= 2 and nd % 2 == 0

    @pl.when(i == 0)
    def _():
        for c in range(seq // tk):
            vt_ref[:A_DV, c * tk:(c + 1) * tk] = v_ref[c * tk:(c + 1) * tk, :].T
        vt_ref[A_DV:, :] = jnp.ones((vt_ref.shape[0] - A_DV, seq), BF16)

    qscale = math.log2(math.e) * A_DQK ** -0.5
    qt = (q_ref[...].astype(F32) * qscale).astype(BF16).T
    sub = lax.broadcasted_iota(jnp.int32, qt.shape, 0)
    zero = jnp.zeros_like(qt)
    qts = (jnp.where(sub < A_DQK, qt, zero), jnp.where(sub >= A_DQK, qt, zero))

    def scores(blk, slot, masked):
        kb = k_ref[pl.ds(pl.multiple_of(blk * tk, tk), tk), :]
        if masked:
            kv_i = blk * tk + lax.broadcasted_iota(jnp.int32, (tk, tq), 0)
            q_i = i * tq + lax.broadcasted_iota(jnp.int32, (tk, tq), 1)
            keep = kv_i <= q_i
        for mp in range(2):
            s = jnp.dot(kb, qts[mp], preferred_element_type=F32)
            if masked:
                s = jnp.where(keep, s, -jnp.inf)
            s_ref[slot, mp] = s

    def softmax(slot, stats):
        out, alphas = [], []
        for mp in range(2):
            m_old = stats[mp]
            s = s_ref[slot, mp]
            m_new = jnp.maximum(m_old, jnp.max(s, axis=0, keepdims=True))
            p_ref[slot, mp] = jnp.exp2(s - m_new).astype(BF16)
            out.append(m_new)
            alphas.append(jnp.exp2(m_old - m_new))
        return tuple(out), tuple(alphas)

    def accumulate(blk, slot, alphas):
        vtb = vt_ref[:, pl.ds(pl.multiple_of(blk * tk, tk), tk)]
        for mp in range(2):
            pv = jnp.dot(vtb, p_ref[slot, mp], preferred_element_type=F32)
            acc_ref[mp] = alphas[mp] * acc_ref[mp] + pv

    neg = jnp.full((1, tq), -jnp.inf, F32)
    acc_ref[...] = jnp.zeros(acc_ref.shape, F32)
    d0 = i * nd

    scores(d0, 0, True)
    scores(d0 + 1, 1, True)
    stats, alphas = softmax(0, (neg, neg))
    blk2, blk1 = d0, d0 + 1
    for k in range(2, nd):
        scores(d0 + k, k % 2, True)
        accumulate(blk2, k % 2, alphas)
        stats, alphas = softmax((k - 1) % 2, stats)
        blk2, blk1 = blk1, d0 + k

    def two_steps(t, carry):
        stats, alphas, blk2, blk1 = carry
        for u in range(2):
            blk = 2 * t + u
            slot = (nd + u) % 2
            scores(blk, slot, False)
            accumulate(blk2, slot, alphas)
            stats, alphas = softmax(1 - slot, stats)
            blk2, blk1 = blk1, blk
        return stats, alphas, blk2, blk1

    nu = i * nd
    stats, alphas, blk2, blk1 = lax.fori_loop(
        0, nu // 2, two_steps, (stats, alphas, blk2, blk1))

    slot = nd % 2
    accumulate(blk2, slot, alphas)
    stats, alphas = softmax(1 - slot, stats)
    accumulate(blk1, 1 - slot, alphas)

    lp = lam_ref[...]
    lam = (jnp.exp(jnp.sum(lp[0:1] * lp[1:2], axis=-1, keepdims=True))
           - jnp.exp(jnp.sum(lp[2:3] * lp[3:4], axis=-1, keepdims=True)) + lam_init)
    o1 = acc_ref[0, :A_DV] / acc_ref[0, A_DV:A_DV + 1]
    o2 = acc_ref[1, :A_DV] / acc_ref[1, A_DV:A_DV + 1]
    ot = o1 - lam * o2
    ms = jnp.mean(ot * ot, axis=0, keepdims=True)
    ot = ot * lax.rsqrt(ms + EPS) * g_ref[...] * (1.0 - lam_init)
    o_ref[...] = (ot.T * jax.nn.silu(gate_ref[...].astype(F32))).astype(o_ref.dtype)


def _diff_attn(p, lam_p, subln_g, *, batch, seq, lam_init, tq, tk):
    t = batch * seq
    nq = seq // tq
    kernel = functools.partial(_diff_attn_kernel, tq=tq, tk=tk, seq=seq, lam_init=lam_init)
    return pl.pallas_call(
        kernel,
        grid=(batch, A_HEADS, nq),
        in_specs=[
            pl.BlockSpec((tq, 128), lambda b, h, i: (b * nq + i, COL_QA + h)),
            pl.BlockSpec((seq, 128), lambda b, h, i: (b, COL_KA + h)),
            pl.BlockSpec((seq, 128), lambda b, h, i: (b, COL_VA + h)),
            pl.BlockSpec((tq, 128), lambda b, h, i: (b * nq + i, COL_GA + h)),
            pl.BlockSpec((4, A_DQK), lambda b, h, i: (0, 0)),
            pl.BlockSpec((A_DV, 1), lambda b, h, i: (0, 0)),
        ],
        out_specs=pl.BlockSpec((tq, 128), lambda b, h, i: (b * nq + i, h)),
        out_shape=jax.ShapeDtypeStruct((t, BRANCH_WIDTH), BF16),
        scratch_shapes=[
            pltpu.VMEM((A_DV + BF16_SUBLANES, seq), BF16),
            pltpu.VMEM((2, 2, tk, tq), F32),
            pltpu.VMEM((2, 2, tk, tq), BF16),
            pltpu.VMEM((2, A_DV + BF16_SUBLANES, tq), F32),
        ],
        compiler_params=pltpu.CompilerParams(
            dimension_semantics=("arbitrary", "arbitrary", "arbitrary"),
            vmem_limit_bytes=VMEM_LIMIT_BYTES),
        name="diff_attn",
    )(p, p, p, p, lam_p, subln_g.reshape(A_DV, 1))


def _hgrn_kernel(f_ref, i_ref, q_ref, gate_ref, lbraw_ref, ng_ref, o_ref,
                 st_ref, b_s, q_s, qe_s, k_s, o_s, *, layer, rows):
    @pl.when(pl.program_id(1) == 0)
    def _():
        st_ref[...] = jnp.zeros(st_ref.shape, F32)

    raw = lbraw_ref[...]
    e = jnp.exp(raw - jnp.max(raw, axis=0, keepdims=True))
    w = e / jnp.sum(e, axis=0, keepdims=True)
    lb = jnp.zeros((1, raw.shape[1]), F32)
    for j in range(1, layer + 1):
        lb = lb + w[j:j + 1]

    z = f_ref[...].astype(F32)
    e = jnp.exp(-jnp.abs(z))
    inv = 1.0 / (1.0 + e)
    log_sig = jnp.minimum(z, 0.0) + jnp.log(inv)
    k_s[...] = (1.0 - lb) * (jnp.where(z >= 0.0, e, 1.0) * inv)
    a = jnp.log(lb)
    c = jnp.log1p(-lb) + log_sig
    logf = jnp.maximum(a, c) + jnp.log(1.0 + jnp.exp(-jnp.abs(a - c)))
    qf = q_ref[...].astype(F32)
    q = qf / (1.0 + jnp.exp(-qf))
    q_s[...] = q

    r = lax.broadcasted_iota(jnp.int32, (rows, rows), 0)
    c = lax.broadcasted_iota(jnp.int32, (rows, rows), 1)
    tri = jnp.where((r // SUB == c // SUB) & (c <= r), 1.0, 0.0).astype(BF16)
    hi = logf.astype(BF16)
    rem = logf - hi.astype(F32)
    mid = rem.astype(BF16)
    lo = (rem - mid.astype(F32)).astype(BF16)
    b = (jnp.dot(tri, hi, preferred_element_type=F32)
         + jnp.dot(tri, mid, preferred_element_type=F32)
         + jnp.dot(tri, lo, preferred_element_type=F32))
    b_s[...] = b
    qe_s[...] = q * jnp.exp(b)

    t_idx = lax.broadcasted_iota(jnp.int32, (SUB, 1), 0)

    def body(n, carry):
        r0 = pl.multiple_of(n * SUB, SUB)
        for h in range(B_HEADS):
            cs = slice(h * B_DK, (h + 1) * B_DK)
            bn = b_s[pl.ds(r0, SUB), cs]
            qn = q_s[pl.ds(r0, SUB), cs]
            kn = k_s[pl.ds(r0, SUB), cs]
            qe = qe_s[pl.ds(r0, SUB), cs]
            vn = i_ref[pl.ds(r0, SUB), cs]
            vf = vn.astype(F32)
            st = st_ref[h]
            acc = _nt_dot(qe.astype(BF16), st.astype(BF16))
            for s in range(SUB):
                arg = jnp.where(t_idx >= s, bn - bn[s:s + 1], -jnp.inf)
                wts = qn * (kn[s:s + 1] * jnp.exp(arg))
                acc = acc + jnp.sum(wts, axis=-1, keepdims=True) * vf[s:s + 1]
            o_s[pl.ds(r0, SUB), cs] = acc
            bl = bn[SUB - 1:SUB]
            kd = kn * jnp.exp(bl - bn)
            st_ref[h] = st * jnp.exp(bl) + _tn_dot(vn, kd.astype(BF16))
        return carry

    lax.fori_loop(0, rows // SUB, body, 0)

    for h in range(B_HEADS):
        cs = slice(h * B_DK, (h + 1) * B_DK)
        o = o_s[:, cs]
        ms = jnp.mean(o * o, axis=-1, keepdims=True)
        y = o * lax.rsqrt(ms + EPS) * ng_ref[...]
        o_ref[:, cs] = (y * jax.nn.silu(gate_ref[:, cs].astype(F32))).astype(o_ref.dtype)


def _hgrn(p, lb_raw, norm_g, *, batch, seq, layer, rows):
    t = batch * seq
    nr = seq // rows
    kernel = functools.partial(_hgrn_kernel, layer=layer, rows=rows)
    blk = lambda col: pl.BlockSpec((rows, 1024), lambda b, i: (b * nr + i, col // 8))
    depth = lb_raw.shape[0]
    return pl.pallas_call(
        kernel,
        grid=(batch, nr),
        in_specs=[
            blk(COL_FB), blk(COL_IB), blk(COL_QB), blk(COL_GB),
            pl.BlockSpec((depth, 1024), lambda b, i: (0, 0)),
            pl.BlockSpec((1, B_DK), lambda b, i: (0, 0)),
        ],
        out_specs=pl.BlockSpec((rows, 1024), lambda b, i: (b * nr + i, 0)),
        out_shape=jax.ShapeDtypeStruct((t, BRANCH_WIDTH), BF16),
        scratch_shapes=[
            pltpu.VMEM((B_HEADS, B_DK, B_DK), F32),
            pltpu.VMEM((rows, 1024), F32),
            pltpu.VMEM((rows, 1024), F32),
            pltpu.VMEM((rows, 1024), F32),
            pltpu.VMEM((rows, 1024), F32),
            pltpu.VMEM((rows, 1024), F32),
        ],
        compiler_params=pltpu.CompilerParams(
            dimension_semantics=("arbitrary", "arbitrary"),
            vmem_limit_bytes=VMEM_LIMIT_BYTES),
        name="hgrn2",
    )(p, p, p, p, lb_raw, norm_g.reshape(1, B_DK))


def _xattn_kernel(q_ref, gate_ref, k_ref, v_ref, o_ref):
    for h in range(C_HEADS):
        cs = slice(h * C_DH, (h + 1) * C_DH)
        s = _nt_dot(q_ref[:, cs], k_ref[:, cs]) * (C_DH ** -0.5)
        m = jnp.max(s, axis=-1, keepdims=True)
        pr = jnp.exp(s - m)
        l = jnp.sum(pr, axis=-1, keepdims=True)
        o = jnp.dot(pr.astype(BF16), v_ref[:, cs], preferred_element_type=F32) / l
        o_ref[:, cs] = (o * jax.nn.silu(gate_ref[:, cs].astype(F32))).astype(o_ref.dtype)


def _xattn(p, kv, *, batch, seq, tq):
    t = batch * seq
    nq = seq // tq
    return pl.pallas_call(
        _xattn_kernel,
        grid=(batch, nq),
        in_specs=[
            pl.BlockSpec((tq, 1024), lambda b, i: (b * nq + i, COL_QC_1024)),
            pl.BlockSpec((tq, 1024), lambda b, i: (b * nq + i, COL_GC_1024)),
            pl.BlockSpec((N_MEM, 1024), lambda b, i: (b, 0)),
            pl.BlockSpec((N_MEM, 1024), lambda b, i: (b, 1)),
        ],
        out_specs=pl.BlockSpec((tq, 1024), lambda b, i: (b * nq + i, 0)),
        out_shape=jax.ShapeDtypeStruct((t, BRANCH_WIDTH), BF16),
        compiler_params=pltpu.CompilerParams(
            dimension_semantics=("arbitrary", "arbitrary"),
            vmem_limit_bytes=VMEM_LIMIT_BYTES),
        name="mem_xattn",
    )(p, p, kv, kv)


def _merge_kernel(oa_ref, ob_ref, oc_ref, g0_ref, g1_ref, g2_ref, x_ref, wb_ref, wo_ref,
                  fg_ref, o_ref, *, final):
    y = None
    for o_r, g_r, j in ((oa_ref, g0_ref, 0), (ob_ref, g1_ref, 1), (oc_ref, g2_ref, 2)):
        term = jax.nn.sigmoid(g_r[...].astype(F32)) * jnp.dot(
            o_r[...], wb_ref[j], preferred_element_type=F32)
        y = term if y is None else y + term
    xn = x_ref[...] + jnp.dot(y.astype(BF16), wo_ref[...], preferred_element_type=F32)
    if final:
        ms = jnp.mean(xn * xn, axis=-1, keepdims=True)
        xn = xn * lax.rsqrt(ms + EPS) * fg_ref[...]
    o_ref[...] = xn


def _merge(oa, ob, oc, p, x, wb, wo, final_g, *, layer, tm, final):
    t = x.shape[0]
    row = lambda i: (i, 0)
    const2 = lambda i: (0, 0)
    kernel = functools.partial(_merge_kernel, final=final)
    return pl.pallas_call(
        kernel,
        grid=(t // tm,),
        in_specs=[
            pl.BlockSpec((tm, BRANCH_WIDTH), row),
            pl.BlockSpec((tm, BRANCH_WIDTH), row),
            pl.BlockSpec((tm, BRANCH_WIDTH), row),
            pl.BlockSpec((tm, D_MODEL), lambda i: (i, COL_GL_2048)),
            pl.BlockSpec((tm, D_MODEL), lambda i: (i, COL_GL_2048 + 1)),
            pl.BlockSpec((tm, D_MODEL), lambda i: (i, COL_GL_2048 + 2)),
            pl.BlockSpec((tm, D_MODEL), row),
            pl.BlockSpec((None, N_BRANCH, BRANCH_WIDTH, D_MODEL), lambda i: (layer, 0, 0, 0),
                         pipeline_mode=pl.Buffered(1)),
            pl.BlockSpec((None, D_MODEL, D_MODEL), lambda i: (layer, 0, 0),
                         pipeline_mode=pl.Buffered(1)),
            pl.BlockSpec((1, D_MODEL), const2),
        ],
        out_specs=pl.BlockSpec((tm, D_MODEL), row),
        out_shape=jax.ShapeDtypeStruct((t, D_MODEL), F32),
        compiler_params=pltpu.CompilerParams(
            dimension_semantics=("arbitrary",),
            vmem_limit_bytes=VMEM_LIMIT_BYTES),
        name="merge_out",
    )(oa, ob, oc, p, p, p, x, wb, wo, final_g.reshape(1, D_MODEL))


def kernel(x, mem, norm_g, w_in, diff_lambda, diff_subln_g, hgrn_lb_raw, hgrn_norm_g,
           mem_norm_g, w_kv_mem, w_branch, w_out, final_norm_g):
    batch, seq, d = x.shape
    depth = w_in.shape[0]
    xt = x.reshape(batch * seq, d)
    memt = mem.reshape(batch * N_MEM, d)
    w_in_b = w_in.astype(BF16)
    w_kv_b = w_kv_mem.astype(BF16)
    w_br_b = w_branch.astype(BF16)
    w_out_b = w_out.astype(BF16)
    lb_raw = hgrn_lb_raw.astype(F32)
    for l in range(depth):
        lam_init = 0.8 - 0.6 * math.exp(-0.3 * l)
        p = _norm_proj(xt, norm_g[l], w_in_b, layer=l, tm=1024, tn=1024)
        kv = _norm_proj(memt, mem_norm_g[l], w_kv_b, layer=l, tm=1024, tn=1024)
        oa = _diff_attn(p, diff_lambda[l], diff_subln_g[l], batch=batch, seq=seq,
                        lam_init=lam_init, tq=512, tk=256)
        ob = _hgrn(p, lb_raw, hgrn_norm_g[l], batch=batch, seq=seq, layer=l, rows=256)
        oc = _xattn(p, kv, batch=batch, seq=seq, tq=512)
        xt = _merge(oa, ob, oc, p, xt, w_br_b, w_out_b, final_norm_g,
                    layer=l, tm=256, final=(l == depth - 1))
    return xt.reshape(batch, seq, d)
```

```python
import functools
import math

import jax
import jax.numpy as jnp
from jax import lax
from jax.experimental import pallas as pl
from jax.experimental.pallas import tpu as pltpu

F32 = jnp.float32
BF16 = jnp.bfloat16

D_MODEL = 2048
N_MEM = 256
A_HEADS = 8
A_DQK = 64
A_DV = 128
B_HEADS = 8
B_DK = 128
C_HEADS = 4
C_DH = 256
BRANCH_WIDTH = 1024
N_BRANCH = 3
EPS = 1e-6

COL_QA, COL_KA, COL_VA, COL_GA = 0, 8, 16, 24
COL_FB, COL_IB, COL_QB, COL_GB = 32, 40, 48, 56
COL_QC_1024, COL_GC_1024 = 8, 9
COL_GL_2048 = 5

VMEM_LIMIT_BYTES = 56 * 1024 * 1024

BF16_SUBLANES = 16
CHUNK = 64
MAX_CHUNK_DECAY = 120.0
SUB = 16


def _nt_dot(a, b):
    return lax.dot_general(a, b, (((1,), (1,)), ((), ())), preferred_element_type=F32)


def _tn_dot(a, b):
    return lax.dot_general(a, b, (((0,), (0,)), ((), ())), preferred_element_type=F32)


def _norm_proj_kernel(x_ref, g_ref, w_ref, o_ref, h_ref):
    @pl.when(pl.program_id(1) == 0)
    def _():
        x = x_ref[...]
        ms = jnp.mean(x * x, axis=-1, keepdims=True)
        h_ref[...] = (x * lax.rsqrt(ms + EPS) * g_ref[...]).astype(BF16)

    o_ref[...] = jnp.dot(h_ref[...], w_ref[...], preferred_element_type=F32).astype(o_ref.dtype)


def _norm_proj(x, g, w, *, layer, tm, tn):
    m, k = x.shape
    n = w.shape[2]
    return pl.pallas_call(
        _norm_proj_kernel,
        grid=(m // tm, n // tn),
        in_specs=[
            pl.BlockSpec((tm, k), lambda i, j: (i, 0)),
            pl.BlockSpec((1, k), lambda i, j: (0, 0)),
            pl.BlockSpec((None, k, tn), lambda i, j: (layer, 0, j)),
        ],
        out_specs=pl.BlockSpec((tm, tn), lambda i, j: (i, j)),
        out_shape=jax.ShapeDtypeStruct((m, n), BF16),
        scratch_shapes=[pltpu.VMEM((tm, k), BF16)],
        compiler_params=pltpu.CompilerParams(
            dimension_semantics=("arbitrary", "arbitrary"),
            vmem_limit_bytes=VMEM_LIMIT_BYTES),
        name="norm_proj",
    )(x, g.reshape(1, k), w)


def _diff_attn_kernel(q_ref, k_ref, v_ref, gate_ref, lam_ref, g_ref, o_ref,
                      vt_ref, s0_ref, s1_ref, p0_ref, p1_ref, acc_ref, *, tq, tk, seq, lam_init):
    i = pl.program_id(2)
    nd = tq // tk
    assert nd >= 2 and nd % 2 == 0
    s_refs = (s0_ref, s1_ref)
    p_refs = (p0_ref, p1_ref)

    @pl.when(i == 0)
    def _():
        for c in range(seq // tk):
            vt_ref[:A_DV, c * tk:(c + 1) * tk] = v_ref[c * tk:(c + 1) * tk, :].T
        vt_ref[A_DV:, :] = jnp.ones((vt_ref.shape[0] - A_DV, seq), BF16)

    qscale = math.log2(math.e) * A_DQK ** -0.5
    qt = (q_ref[...].astype(F32) * qscale).astype(BF16).T
    sub = lax.broadcasted_iota(jnp.int32, qt.shape, 0)
    zero = jnp.zeros_like(qt)
    qts = (jnp.where(sub < A_DQK, qt, zero), jnp.where(sub >= A_DQK, qt, zero))

    def scores(blk, slot, masked):
        kb = k_ref[pl.ds(pl.multiple_of(blk * tk, tk), tk), :]
        if masked:
            kv_i = blk * tk + lax.broadcasted_iota(jnp.int32, (tk, tq), 0)
            q_i = i * tq + lax.broadcasted_iota(jnp.int32, (tk, tq), 1)
            keep = kv_i <= q_i
        for mp in range(2):
            s = jnp.dot(kb, qts[mp], preferred_element_type=F32)
            if masked:
                s = jnp.where(keep, s, -jnp.inf)
            s_refs[slot][mp] = s

    def softmax(slot, stats):
        out, alphas = [], []
        for mp in range(2):
            m_old = stats[mp]
            s = s_refs[slot][mp]
            m_new = jnp.maximum(m_old, jnp.max(s, axis=0, keepdims=True))
            p_refs[slot][mp] = jnp.exp2(s - m_new).astype(BF16)
            out.append(m_new)
            alphas.append(jnp.exp2(m_old - m_new))
        return tuple(out), tuple(alphas)

    def accumulate(blk, slot, alphas):
        vtb = vt_ref[:, pl.ds(pl.multiple_of(blk * tk, tk), tk)]
        for mp in range(2):
            pv = jnp.dot(vtb, p_refs[slot][mp], preferred_element_type=F32)
            acc_ref[mp] = alphas[mp] * acc_ref[mp] + pv

    neg = jnp.full((1, tq), -jnp.inf, F32)
    acc_ref[...] = jnp.zeros(acc_ref.shape, F32)
    d0 = i * nd

    scores(d0, 0, True)
    scores(d0 + 1, 1, True)
    stats, alphas = softmax(0, (neg, neg))
    blk2, blk1 = d0, d0 + 1
    for k in range(2, nd):
        scores(d0 + k, k % 2, True)
        accumulate(blk2, k % 2, alphas)
        stats, alphas = softmax((k - 1) % 2, stats)
        blk2, blk1 = blk1, d0 + k

    def two_steps(t, carry):
        stats, alphas, blk2, blk1 = carry
        for u in range(2):
            blk = 2 * t + u
            slot = (nd + u) % 2
            new_stats, new_alphas = softmax(1 - slot, stats)
            scores(blk, slot, False)
            accumulate(blk2, slot, alphas)
            stats, alphas = new_stats, new_alphas
            blk2, blk1 = blk1, blk
        return stats, alphas, blk2, blk1

    nu = i * nd
    stats, alphas, blk2, blk1 = lax.fori_loop(
        0, nu // 2, two_steps, (stats, alphas, blk2, blk1))

    slot = nd % 2
    accumulate(blk2, slot, alphas)
    stats, alphas = softmax(1 - slot, stats)
    accumulate(blk1, 1 - slot, alphas)

    lp = lam_ref[...]
    lam = (jnp.exp(jnp.sum(lp[0:1] * lp[1:2], axis=-1, keepdims=True))
           - jnp.exp(jnp.sum(lp[2:3] * lp[3:4], axis=-1, keepdims=True)) + lam_init)
    o1 = acc_ref[0, :A_DV] / acc_ref[0, A_DV:A_DV + 1]
    o2 = acc_ref[1, :A_DV] / acc_ref[1, A_DV:A_DV + 1]
    ot = o1 - lam * o2
    ms = jnp.mean(ot * ot, axis=0, keepdims=True)
    ot = ot * lax.rsqrt(ms + EPS) * g_ref[...] * (1.0 - lam_init)
    o_ref[...] = (ot.T * jax.nn.silu(gate_ref[...].astype(F32))).astype(o_ref.dtype)


def _diff_attn(p, lam_p, subln_g, *, batch, seq, lam_init, tq, tk):
    t = batch * seq
    nq = seq // tq
    kernel = functools.partial(_diff_attn_kernel, tq=tq, tk=tk, seq=seq, lam_init=lam_init)
    return pl.pallas_call(
        kernel,
        grid=(batch, A_HEADS, nq),
        in_specs=[
            pl.BlockSpec((tq, 128), lambda b, h, i: (b * nq + i, COL_QA + h)),
            pl.BlockSpec((seq, 128), lambda b, h, i: (b, COL_KA + h)),
            pl.BlockSpec((seq, 128), lambda b, h, i: (b, COL_VA + h)),
            pl.BlockSpec((tq, 128), lambda b, h, i: (b * nq + i, COL_GA + h)),
            pl.BlockSpec((4, A_DQK), lambda b, h, i: (0, 0)),
            pl.BlockSpec((A_DV, 1), lambda b, h, i: (0, 0)),
        ],
        out_specs=pl.BlockSpec((tq, 128), lambda b, h, i: (b * nq + i, h)),
        out_shape=jax.ShapeDtypeStruct((t, BRANCH_WIDTH), BF16),
        scratch_shapes=[
            pltpu.VMEM((A_DV + BF16_SUBLANES, seq), BF16),
            pltpu.VMEM((2, tk, tq), F32),
            pltpu.VMEM((2, tk, tq), F32),
            pltpu.VMEM((2, tk, tq), BF16),
            pltpu.VMEM((2, tk, tq), BF16),
            pltpu.VMEM((2, A_DV + BF16_SUBLANES, tq), F32),
        ],
        compiler_params=pltpu.CompilerParams(
            dimension_semantics=("arbitrary", "arbitrary", "arbitrary"),
            vmem_limit_bytes=VMEM_LIMIT_BYTES),
        name="diff_attn",
    )(p, p, p, p, lam_p, subln_g.reshape(A_DV, 1))


def _block_cumsum(x, rows, blk):
    r = lax.broadcasted_iota(jnp.int32, (rows, rows), 0)
    c = lax.broadcasted_iota(jnp.int32, (rows, rows), 1)
    tri = jnp.where((r // blk == c // blk) & (c <= r), 1.0, 0.0).astype(BF16)
    hi = x.astype(BF16)
    rem = x - hi.astype(F32)
    mid = rem.astype(BF16)
    lo = (rem - mid.astype(F32)).astype(BF16)
    return (jnp.dot(tri, hi, preferred_element_type=F32)
            + jnp.dot(tri, mid, preferred_element_type=F32)
            + jnp.dot(tri, lo, preferred_element_type=F32))


def _hgrn_kernel(f_ref, i_ref, q_ref, gate_ref, lbraw_ref, ng_ref, o_ref,
                 st_ref, b_s, q_s, lf_s, k_s, o_s, *, layer, rows):
    @pl.when(pl.program_id(1) == 0)
    def _():
        st_ref[...] = jnp.zeros(st_ref.shape, F32)

    raw = lbraw_ref[...]
    e = jnp.exp(raw - jnp.max(raw, axis=0, keepdims=True))
    w = e / jnp.sum(e, axis=0, keepdims=True)
    lb = jnp.zeros((1, raw.shape[1]), F32)
    for j in range(1, layer + 1):
        lb = lb + w[j:j + 1]

    z = f_ref[...].astype(F32)
    e = jnp.exp(-jnp.abs(z))
    inv = 1.0 / (1.0 + e)
    log_sig = jnp.minimum(z, 0.0) + jnp.log(inv)
    k_s[...] = (1.0 - lb) * (jnp.where(z >= 0.0, e, 1.0) * inv)
    a = jnp.log(lb)
    c = jnp.log1p(-lb) + log_sig
    logf = jnp.maximum(a, c) + jnp.log(1.0 + jnp.exp(-jnp.abs(a - c)))
    qf = q_ref[...].astype(F32)
    q_s[...] = qf / (1.0 + jnp.exp(-qf))
    lf_s[...] = logf

    b_s[...] = _block_cumsum(logf, rows, CHUNK)
    totals = jnp.concatenate(
        [b_s[c * CHUNK + CHUNK - 1:(c + 1) * CHUNK, :] for c in range(rows // CHUNK)], axis=0)
    fast = jnp.max(-totals) <= MAX_CHUNK_DECAY

    @pl.when(fast)
    def _():
        t_i = lax.broadcasted_iota(jnp.int32, (CHUNK, CHUNK), 0)
        s_i = lax.broadcasted_iota(jnp.int32, (CHUNK, CHUNK), 1)
        causal = s_i <= t_i
        for c in range(rows // CHUNK):
            rs = slice(c * CHUNK, (c + 1) * CHUNK)
            for h in range(B_HEADS):
                cs = slice(h * B_DK, (h + 1) * B_DK)
                bn, qn, kn, vn = b_s[rs, cs], q_s[rs, cs], k_s[rs, cs], i_ref[rs, cs]
                bl = bn[CHUNK - 1:CHUNK]
                half = 0.5 * bl
                qp = (qn * jnp.exp(bn - half)).astype(BF16)
                kp = (kn * jnp.exp(half - bn)).astype(BF16)
                a = jnp.where(causal, _nt_dot(qp, kp), 0.0).astype(BF16)
                st = st_ref[h]
                qe = (qn * jnp.exp(bn)).astype(BF16)
                o_s[rs, cs] = (jnp.dot(a, vn, preferred_element_type=F32)
                               + _nt_dot(qe, st.astype(BF16)))
                kd = (kn * jnp.exp(bl - bn)).astype(BF16)
                st_ref[h] = st * jnp.exp(bl) + _tn_dot(vn, kd)

    @pl.when(jnp.logical_not(fast))
    def _():
        b_s[...] = _block_cumsum(lf_s[...], rows, SUB)
        t_idx = lax.broadcasted_iota(jnp.int32, (SUB, 1), 0)

        def body(n, carry):
            r0 = pl.multiple_of(n * SUB, SUB)
            for h in range(B_HEADS):
                cs = slice(h * B_DK, (h + 1) * B_DK)
                bn = b_s[pl.ds(r0, SUB), cs]
                qn = q_s[pl.ds(r0, SUB), cs]
                kn = k_s[pl.ds(r0, SUB), cs]
                vn = i_ref[pl.ds(r0, SUB), cs]
                vf = vn.astype(F32)
                st = st_ref[h]
                acc = _nt_dot((qn * jnp.exp(bn)).astype(BF16), st.astype(BF16))
                for s in range(SUB):
                    arg = jnp.where(t_idx >= s, bn - bn[s:s + 1], -jnp.inf)
                    wts = qn * (kn[s:s + 1] * jnp.exp(arg))
                    acc = acc + jnp.sum(wts, axis=-1, keepdims=True) * vf[s:s + 1]
                o_s[pl.ds(r0, SUB), cs] = acc
                bl = bn[SUB - 1:SUB]
                kd = kn * jnp.exp(bl - bn)
                st_ref[h] = st * jnp.exp(bl) + _tn_dot(vn, kd.astype(BF16))
            return carry

        lax.fori_loop(0, rows // SUB, body, 0)

    for h in range(B_HEADS):
        cs = slice(h * B_DK, (h + 1) * B_DK)
        o = o_s[:, cs]
        ms = jnp.mean(o * o, axis=-1, keepdims=True)
        y = o * lax.rsqrt(ms + EPS) * ng_ref[...]
        o_ref[:, cs] = (y * jax.nn.silu(gate_ref[:, cs].astype(F32))).astype(o_ref.dtype)


def _hgrn(p, lb_raw, norm_g, *, batch, seq, layer, rows):
    t = batch * seq
    nr = seq // rows
    kernel = functools.partial(_hgrn_kernel, layer=layer, rows=rows)
    blk = lambda col: pl.BlockSpec((rows, 1024), lambda b, i: (b * nr + i, col // 8))
    depth = lb_raw.shape[0]
    return pl.pallas_call(
        kernel,
        grid=(batch, nr),
        in_specs=[
            blk(COL_FB), blk(COL_IB), blk(COL_QB), blk(COL_GB),
            pl.BlockSpec((depth, 1024), lambda b, i: (0, 0)),
            pl.BlockSpec((1, B_DK), lambda b, i: (0, 0)),
        ],
        out_specs=pl.BlockSpec((rows, 1024), lambda b, i: (b * nr + i, 0)),
        out_shape=jax.ShapeDtypeStruct((t, BRANCH_WIDTH), BF16),
        scratch_shapes=[
            pltpu.VMEM((B_HEADS, B_DK, B_DK), F32),
            pltpu.VMEM((rows, 1024), F32),
            pltpu.VMEM((rows, 1024), F32),
            pltpu.VMEM((rows, 1024), F32),
            pltpu.VMEM((rows, 1024), F32),
            pltpu.VMEM((rows, 1024), F32),
        ],
        compiler_params=pltpu.CompilerParams(
            dimension_semantics=("arbitrary", "arbitrary"),
            vmem_limit_bytes=VMEM_LIMIT_BYTES),
        name="hgrn2",
    )(p, p, p, p, lb_raw, norm_g.reshape(1, B_DK))


def _xattn_kernel(q_ref, gate_ref, k_ref, v_ref, o_ref):
    for h in range(C_HEADS):
        cs = slice(h * C_DH, (h + 1) * C_DH)
        s = _nt_dot(q_ref[:, cs], k_ref[:, cs]) * (C_DH ** -0.5)
        m = jnp.max(s, axis=-1, keepdims=True)
        pr = jnp.exp(s - m)
        l = jnp.sum(pr, axis=-1, keepdims=True)
        o = jnp.dot(pr.astype(BF16), v_ref[:, cs], preferred_element_type=F32) / l
        o_ref[:, cs] = (o * jax.nn.silu(gate_ref[:, cs].astype(F32))).astype(o_ref.dtype)


def _xattn(p, kv, *, batch, seq, tq):
    t = batch * seq
    nq = seq // tq
    return pl.pallas_call(
        _xattn_kernel,
        grid=(batch, nq),
        in_specs=[
            pl.BlockSpec((tq, 1024), lambda b, i: (b * nq + i, COL_QC_1024)),
            pl.BlockSpec((tq, 1024), lambda b, i: (b * nq + i, COL_GC_1024)),
            pl.BlockSpec((N_MEM, 1024), lambda b, i: (b, 0)),
            pl.BlockSpec((N_MEM, 1024), lambda b, i: (b, 1)),
        ],
        out_specs=pl.BlockSpec((tq, 1024), lambda b, i: (b * nq + i, 0)),
        out_shape=jax.ShapeDtypeStruct((t, BRANCH_WIDTH), BF16),
        compiler_params=pltpu.CompilerParams(
            dimension_semantics=("arbitrary", "arbitrary"),
            vmem_limit_bytes=VMEM_LIMIT_BYTES),
        name="mem_xattn",
    )(p, p, kv, kv)


def _merge_kernel(oa_ref, ob_ref, oc_ref, g0_ref, g1_ref, g2_ref, x_ref, wb_ref, wo_ref,
                  fg_ref, o_ref, *, final):
    y = None
    for o_r, g_r, j in ((oa_ref, g0_ref, 0), (ob_ref, g1_ref, 1), (oc_ref, g2_ref, 2)):
        term = jax.nn.sigmoid(g_r[...].astype(F32)) * jnp.dot(
            o_r[...], wb_ref[j], preferred_element_type=F32)
        y = term if y is None else y + term
    xn = x_ref[...] + jnp.dot(y.astype(BF16), wo_ref[...], preferred_element_type=F32)
    if final:
        ms = jnp.mean(xn * xn, axis=-1, keepdims=True)
        xn = xn * lax.rsqrt(ms + EPS) * fg_ref[...]
    o_ref[...] = xn


def _merge(oa, ob, oc, p, x, wb, wo, final_g, *, layer, tm, final):
    t = x.shape[0]
    row = lambda i: (i, 0)
    const2 = lambda i: (0, 0)
    kernel = functools.partial(_merge_kernel, final=final)
    return pl.pallas_call(
        kernel,
        grid=(t // tm,),
        in_specs=[
            pl.BlockSpec((tm, BRANCH_WIDTH), row),
            pl.BlockSpec((tm, BRANCH_WIDTH), row),
            pl.BlockSpec((tm, BRANCH_WIDTH), row),
            pl.BlockSpec((tm, D_MODEL), lambda i: (i, COL_GL_2048)),
            pl.BlockSpec((tm, D_MODEL), lambda i: (i, COL_GL_2048 + 1)),
            pl.BlockSpec((tm, D_MODEL), lambda i: (i, COL_GL_2048 + 2)),
            pl.BlockSpec((tm, D_MODEL), row),
            pl.BlockSpec((None, N_BRANCH, BRANCH_WIDTH, D_MODEL), lambda i: (layer, 0, 0, 0),
                         pipeline_mode=pl.Buffered(1)),
            pl.BlockSpec((None, D_MODEL, D_MODEL), lambda i: (layer, 0, 0),
                         pipeline_mode=pl.Buffered(1)),
            pl.BlockSpec((1, D_MODEL), const2),
        ],
        out_specs=pl.BlockSpec((tm, D_MODEL), row),
        out_shape=jax.ShapeDtypeStruct((t, D_MODEL), F32),
        compiler_params=pltpu.CompilerParams(
            dimension_semantics=("arbitrary",),
            vmem_limit_bytes=VMEM_LIMIT_BYTES),
        name="merge_out",
    )(oa, ob, oc, p, p, p, x, wb, wo, final_g.reshape(1, D_MODEL))


def kernel(x, mem, norm_g, w_in, diff_lambda, diff_subln_g, hgrn_lb_raw, hgrn_norm_g,
           mem_norm_g, w_kv_mem, w_branch, w_out, final_norm_g):
    batch, seq, d = x.shape
    depth = w_in.shape[0]
    xt = x.reshape(batch * seq, d)
    memt = mem.reshape(batch * N_MEM, d)
    w_in_b = w_in.astype(BF16)
    w_kv_b = w_kv_mem.astype(BF16)
    w_br_b = w_branch.astype(BF16)
    w_out_b = w_out.astype(BF16)
    lb_raw = hgrn_lb_raw.astype(F32)
    for l in range(depth):
        lam_init = 0.8 - 0.6 * math.exp(-0.3 * l)
        p = _norm_proj(xt, norm_g[l], w_in_b, layer=l, tm=1024, tn=1024)
        kv = _norm_proj(memt, mem_norm_g[l], w_kv_b, layer=l, tm=1024, tn=1024)
        oa = _diff_attn(p, diff_lambda[l], diff_subln_g[l], batch=batch, seq=seq,
                        lam_init=lam_init, tq=512, tk=256)
        ob = _hgrn(p, lb_raw, hgrn_norm_g[l], batch=batch, seq=seq, layer=l, rows=256)
        oc = _xattn(p, kv, batch=batch, seq=seq, tq=512)
        xt = _merge(oa, ob, oc, p, xt, w_br_b, w_out_b, final_norm_g,
                    layer=l, tm=256, final=(l == depth - 1))
    return xt.reshape(batch, seq, d)
```

```python
import functools
import math

import jax
import jax.numpy as jnp
import numpy as np
from jax import lax
from jax.experimental import pallas as pl
from jax.experimental.pallas import tpu as pltpu

F32 = jnp.float32
BF16 = jnp.bfloat16

D_MODEL = 2048
N_MEM = 256
A_HEADS = 8
A_DQK = 64
A_DV = 128
B_HEADS = 8
B_DK = 128
C_HEADS = 4
C_DH = 256
BRANCH_WIDTH = 1024
N_BRANCH = 3
EPS = 1e-6

COL_QA, COL_KA, COL_VA, COL_GA = 0, 8, 16, 24
COL_FB, COL_IB, COL_QB, COL_GB = 32, 40, 48, 56
COL_QC_1024, COL_GC_1024 = 8, 9
COL_GL_2048 = 5

VMEM_LIMIT_BYTES = 56 * 1024 * 1024

BF16_SUBLANES = 16
CHUNK = 64
MAX_CHUNK_DECAY = 120.0
SUB = 16


def _nt_dot(a, b):
    return lax.dot_general(a, b, (((1,), (1,)), ((), ())), preferred_element_type=F32)


def _tn_dot(a, b):
    return lax.dot_general(a, b, (((0,), (0,)), ((), ())), preferred_element_type=F32)


def _norm_proj_kernel(x_ref, g_ref, w_ref, o_ref, h_ref):
    @pl.when(pl.program_id(1) == 0)
    def _():
        x = x_ref[...]
        ms = jnp.mean(x * x, axis=-1, keepdims=True)
        h_ref[...] = (x * lax.rsqrt(ms + EPS) * g_ref[...]).astype(BF16)

    o_ref[...] = jnp.dot(h_ref[...], w_ref[...], preferred_element_type=F32).astype(o_ref.dtype)


def _norm_proj(x, g, w, *, layer, tm, tn):
    m, k = x.shape
    n = w.shape[2]
    return pl.pallas_call(
        _norm_proj_kernel,
        grid=(m // tm, n // tn),
        in_specs=[
            pl.BlockSpec((tm, k), lambda i, j: (i, 0)),
            pl.BlockSpec((1, k), lambda i, j: (0, 0)),
            pl.BlockSpec((None, k, tn), lambda i, j: (layer, 0, j)),
        ],
        out_specs=pl.BlockSpec((tm, tn), lambda i, j: (i, j)),
        out_shape=jax.ShapeDtypeStruct((m, n), BF16),
        scratch_shapes=[pltpu.VMEM((tm, k), BF16)],
        compiler_params=pltpu.CompilerParams(
            dimension_semantics=("arbitrary", "arbitrary"),
            vmem_limit_bytes=VMEM_LIMIT_BYTES),
        name="norm_proj",
    )(x, g.reshape(1, k), w)


def _attn_schedule(nq, nd):
    chains = []
    for c in range(nq // 2):
        own = (c, nq - 1 - c)
        steps = [(i, i * nd + d, s) for s, i in enumerate(own) for d in range(nd)]
        steps += [(i, j, s) for s, i in enumerate(own) for j in range(i * nd)]
        chains.append(steps)
    assert len({len(ch) for ch in chains}) == 1
    return np.asarray(chains, np.int32).transpose(1, 0, 2)


def _diff_attn_kernel(tab_ref, q_ref, k_ref, v_ref, gate_ref, lam_ref, g_ref, o_ref,
                      vt_ref, qt0_ref, qt1_ref, s0_ref, s1_ref, m_ref, acc_ref,
                      *, tq, tk, seq, lam_init):
    nq, nd = seq // tq, tq // tk
    n_chains = nq // 2
    qt_refs = (qt0_ref, qt1_ref)

    for c in range(seq // tk):
        vt_ref[:A_DV, c * tk:(c + 1) * tk] = v_ref[c * tk:(c + 1) * tk, :].T
    vt_ref[A_DV:, :] = jnp.ones((vt_ref.shape[0] - A_DV, seq), BF16)

    qscale = math.log2(math.e) * A_DQK ** -0.5
    sub = lax.broadcasted_iota(jnp.int32, (2 * A_DQK, tq), 0)
    for c in range(nq):
        qt = (q_ref[c * tq:(c + 1) * tq, :].astype(F32) * qscale).astype(BF16).T
        zero = jnp.zeros_like(qt)
        qt0_ref[:, c * tq:(c + 1) * tq] = jnp.where(sub < A_DQK, qt, zero)
        qt1_ref[:, c * tq:(c + 1) * tq] = jnp.where(sub >= A_DQK, qt, zero)

    m_ref[...] = jnp.full(m_ref.shape, -jnp.inf, F32)
    acc_ref[...] = jnp.zeros(acc_ref.shape, F32)

    s_refs = (s0_ref, s1_ref)

    def scores(t, par, masked):
        if masked:
            r_minus_c = (lax.broadcasted_iota(jnp.int32, (tk, tq), 0)
                         - lax.broadcasted_iota(jnp.int32, (tk, tq), 1))
        for c in range(n_chains):
            q0 = pl.multiple_of(tab_ref[t, c, 0] * tq, tq)
            k0 = pl.multiple_of(tab_ref[t, c, 1] * tk, tk)
            kb = k_ref[pl.ds(k0, tk), :]
            if masked:
                keep = r_minus_c <= q0 - k0
            for mp in range(2):
                s = jnp.dot(kb, qt_refs[mp][:, pl.ds(q0, tq)],
                            preferred_element_type=F32)
                if masked:
                    s = jnp.where(keep, s, -jnp.inf)
                s_refs[par][c, mp] = s

    def update(t, par):
        for c in range(n_chains):
            own = tab_ref[t, c, 2]
            k0 = pl.multiple_of(tab_ref[t, c, 1] * tk, tk)
            vtb = vt_ref[:, pl.ds(k0, tk)]
            for mp in range(2):
                s = s_refs[par][c, mp]
                m_old = m_ref[c, own, mp]
                m_new = jnp.maximum(m_old, jnp.max(s, axis=0, keepdims=True))
                p = jnp.exp2(s - m_new).astype(BF16)
                pv = jnp.dot(vtb, p, preferred_element_type=F32)
                acc_ref[c, own, mp] = jnp.exp2(m_old - m_new) * acc_ref[c, own, mp] + pv
                m_ref[c, own, mp] = m_new

    n_masked = 2 * nd
    n_trips = tab_ref.shape[0]
    assert n_masked % 2 == 0 and n_trips % 2 == 0
    scores(0, 0, True)
    for t in range(n_masked - 1):
        scores(t + 1, (t + 1) % 2, True)
        update(t, t % 2)

    def two_trips(u, carry):
        for v in range(2):
            t = n_masked - 1 + 2 * u + v
            par = (n_masked - 1 + v) % 2
            scores(t + 1, 1 - par, False)
            update(t, par)
        return carry

    lax.fori_loop(0, (n_trips - n_masked) // 2, two_trips, 0)
    update(n_trips - 1, (n_trips - 1) % 2)

    lp = lam_ref[...]
    lam = (jnp.exp(jnp.sum(lp[0:1] * lp[1:2], axis=-1, keepdims=True))
           - jnp.exp(jnp.sum(lp[2:3] * lp[3:4], axis=-1, keepdims=True)) + lam_init)
    for c in range(n_chains):
        for own, qb in enumerate((c, nq - 1 - c)):
            o1 = acc_ref[c, own, 0, :A_DV] / acc_ref[c, own, 0, A_DV:A_DV + 1]
            o2 = acc_ref[c, own, 1, :A_DV] / acc_ref[c, own, 1, A_DV:A_DV + 1]
            ot = o1 - lam * o2
            ms = jnp.mean(ot * ot, axis=0, keepdims=True)
            ot = ot * lax.rsqrt(ms + EPS) * g_ref[...] * (1.0 - lam_init)
            rows = slice(qb * tq, (qb + 1) * tq)
            o_ref[rows, :] = (ot.T * jax.nn.silu(gate_ref[rows, :].astype(F32))
                              ).astype(o_ref.dtype)


def _diff_attn(p, lam_p, subln_g, *, batch, seq, lam_init, tq, tk):
    t = batch * seq
    nq, nd = seq // tq, tq // tk
    assert nq % 2 == 0
    table = _attn_schedule(nq, nd)
    kernel = functools.partial(_diff_attn_kernel, tq=tq, tk=tk, seq=seq, lam_init=lam_init)
    acc_rows = A_DV + BF16_SUBLANES
    return pl.pallas_call(
        kernel,
        grid=(batch, A_HEADS),
        in_specs=[
            pl.BlockSpec(memory_space=pltpu.SMEM),
            pl.BlockSpec((seq, 128), lambda b, h: (b, COL_QA + h)),
            pl.BlockSpec((seq, 128), lambda b, h: (b, COL_KA + h)),
            pl.BlockSpec((seq, 128), lambda b, h: (b, COL_VA + h)),
            pl.BlockSpec((seq, 128), lambda b, h: (b, COL_GA + h)),
            pl.BlockSpec((4, A_DQK), lambda b, h: (0, 0)),
            pl.BlockSpec((A_DV, 1), lambda b, h: (0, 0)),
        ],
        out_specs=pl.BlockSpec((seq, 128), lambda b, h: (b, h)),
        out_shape=jax.ShapeDtypeStruct((t, BRANCH_WIDTH), BF16),
        scratch_shapes=[
            pltpu.VMEM((acc_rows, seq), BF16),
            pltpu.VMEM((2 * A_DQK, seq), BF16),
            pltpu.VMEM((2 * A_DQK, seq), BF16),
            pltpu.VMEM((nq // 2, 2, tk, tq), F32),
            pltpu.VMEM((nq // 2, 2, tk, tq), F32),
            pltpu.VMEM((nq // 2, 2, 2, 1, tq), F32),
            pltpu.VMEM((nq // 2, 2, 2, acc_rows, tq), F32),
        ],
        compiler_params=pltpu.CompilerParams(
            dimension_semantics=("arbitrary", "arbitrary"),
            vmem_limit_bytes=VMEM_LIMIT_BYTES),
        name="diff_attn",
    )(jnp.asarray(table), p, p, p, p, lam_p, subln_g.reshape(A_DV, 1))


def _block_cumsum(x, rows, blk):
    r = lax.broadcasted_iota(jnp.int32, (rows, rows), 0)
    c = lax.broadcasted_iota(jnp.int32, (rows, rows), 1)
    tri = jnp.where((r // blk == c // blk) & (c <= r), 1.0, 0.0).astype(BF16)
    hi = x.astype(BF16)
    rem = x - hi.astype(F32)
    mid = rem.astype(BF16)
    lo = (rem - mid.astype(F32)).astype(BF16)
    return (jnp.dot(tri, hi, preferred_element_type=F32)
            + jnp.dot(tri, mid, preferred_element_type=F32)
            + jnp.dot(tri, lo, preferred_element_type=F32))


def _hgrn_kernel(f_ref, i_ref, q_ref, gate_ref, lbraw_ref, ng_ref, o_ref,
                 st_ref, b_s, q_s, lf_s, k_s, o_s, *, layer, rows):
    @pl.when(pl.program_id(1) == 0)
    def _():
        st_ref[...] = jnp.zeros(st_ref.shape, F32)

    raw = lbraw_ref[...]
    e = jnp.exp(raw - jnp.max(raw, axis=0, keepdims=True))
    w = e / jnp.sum(e, axis=0, keepdims=True)
    lb = jnp.zeros((1, raw.shape[1]), F32)
    for j in range(1, layer + 1):
        lb = lb + w[j:j + 1]

    z = f_ref[...].astype(F32)
    e = jnp.exp(-jnp.abs(z))
    inv = 1.0 / (1.0 + e)
    log_sig = jnp.minimum(z, 0.0) + jnp.log(inv)
    k_s[...] = (1.0 - lb) * (jnp.where(z >= 0.0, e, 1.0) * inv)
    a = jnp.log(lb)
    c = jnp.log1p(-lb) + log_sig
    logf = jnp.maximum(a, c) + jnp.log(1.0 + jnp.exp(-jnp.abs(a - c)))
    qf = q_ref[...].astype(F32)
    q_s[...] = qf / (1.0 + jnp.exp(-qf))
    lf_s[...] = logf

    b_s[...] = _block_cumsum(logf, rows, CHUNK)
    totals = jnp.concatenate(
        [b_s[c * CHUNK + CHUNK - 1:(c + 1) * CHUNK, :] for c in range(rows // CHUNK)], axis=0)
    fast = jnp.max(-totals) <= MAX_CHUNK_DECAY

    @pl.when(fast)
    def _():
        t_i = lax.broadcasted_iota(jnp.int32, (CHUNK, CHUNK), 0)
        s_i = lax.broadcasted_iota(jnp.int32, (CHUNK, CHUNK), 1)
        causal = s_i <= t_i
        for c in range(rows // CHUNK):
            rs = slice(c * CHUNK, (c + 1) * CHUNK)
            for h in range(B_HEADS):
                cs = slice(h * B_DK, (h + 1) * B_DK)
                bn, qn, kn, vn = b_s[rs, cs], q_s[rs, cs], k_s[rs, cs], i_ref[rs, cs]
                bl = bn[CHUNK - 1:CHUNK]
                half = 0.5 * bl
                qp = (qn * jnp.exp(bn - half)).astype(BF16)
                kp = (kn * jnp.exp(half - bn)).astype(BF16)
                a = jnp.where(causal, _nt_dot(qp, kp), 0.0).astype(BF16)
                st = st_ref[h]
                qe = (qn * jnp.exp(bn)).astype(BF16)
                o_s[rs, cs] = (jnp.dot(a, vn, preferred_element_type=F32)
                               + _nt_dot(qe, st.astype(BF16)))
                kd = (kn * jnp.exp(bl - bn)).astype(BF16)
                st_ref[h] = st * jnp.exp(bl) + _tn_dot(vn, kd)

    @pl.when(jnp.logical_not(fast))
    def _():
        b_s[...] = _block_cumsum(lf_s[...], rows, SUB)
        t_idx = lax.broadcasted_iota(jnp.int32, (SUB, 1), 0)

        def body(n, carry):
            r0 = pl.multiple_of(n * SUB, SUB)
            for h in range(B_HEADS):
                cs = slice(h * B_DK, (h + 1) * B_DK)
                bn = b_s[pl.ds(r0, SUB), cs]
                qn = q_s[pl.ds(r0, SUB), cs]
                kn = k_s[pl.ds(r0, SUB), cs]
                vn = i_ref[pl.ds(r0, SUB), cs]
                vf = vn.astype(F32)
                st = st_ref[h]
                acc = _nt_dot((qn * jnp.exp(bn)).astype(BF16), st.astype(BF16))
                for s in range(SUB):
                    arg = jnp.where(t_idx >= s, bn - bn[s:s + 1], -jnp.inf)
                    wts = qn * (kn[s:s + 1] * jnp.exp(arg))
                    acc = acc + jnp.sum(wts, axis=-1, keepdims=True) * vf[s:s + 1]
                o_s[pl.ds(r0, SUB), cs] = acc
                bl = bn[SUB - 1:SUB]
                kd = kn * jnp.exp(bl - bn)
                st_ref[h] = st * jnp.exp(bl) + _tn_dot(vn, kd.astype(BF16))
            return carry

        lax.fori_loop(0, rows // SUB, body, 0)

    for h in range(B_HEADS):
        cs = slice(h * B_DK, (h + 1) * B_DK)
        o = o_s[:, cs]
        ms = jnp.mean(o * o, axis=-1, keepdims=True)
        y = o * lax.rsqrt(ms + EPS) * ng_ref[...]
        o_ref[:, cs] = (y * jax.nn.silu(gate_ref[:, cs].astype(F32))).astype(o_ref.dtype)


def _hgrn(p, lb_raw, norm_g, *, batch, seq, layer, rows):
    t = batch * seq
    nr = seq // rows
    kernel = functools.partial(_hgrn_kernel, layer=layer, rows=rows)
    blk = lambda col: pl.BlockSpec((rows, 1024), lambda b, i: (b * nr + i, col // 8))
    depth = lb_raw.shape[0]
    return pl.pallas_call(
        kernel,
        grid=(batch, nr),
        in_specs=[
            blk(COL_FB), blk(COL_IB), blk(COL_QB), blk(COL_GB),
            pl.BlockSpec((depth, 1024), lambda b, i: (0, 0)),
            pl.BlockSpec((1, B_DK), lambda b, i: (0, 0)),
        ],
        out_specs=pl.BlockSpec((rows, 1024), lambda b, i: (b * nr + i, 0)),
        out_shape=jax.ShapeDtypeStruct((t, BRANCH_WIDTH), BF16),
        scratch_shapes=[
            pltpu.VMEM((B_HEADS, B_DK, B_DK), F32),
            pltpu.VMEM((rows, 1024), F32),
            pltpu.VMEM((rows, 1024), F32),
            pltpu.VMEM((rows, 1024), F32),
            pltpu.VMEM((rows, 1024), F32),
            pltpu.VMEM((rows, 1024), F32),
        ],
        compiler_params=pltpu.CompilerParams(
            dimension_semantics=("arbitrary", "arbitrary"),
            vmem_limit_bytes=VMEM_LIMIT_BYTES),
        name="hgrn2",
    )(p, p, p, p, lb_raw, norm_g.reshape(1, B_DK))


def _xattn_kernel(q_ref, gate_ref, k_ref, v_ref, o_ref):
    for h in range(C_HEADS):
        cs = slice(h * C_DH, (h + 1) * C_DH)
        s = _nt_dot(q_ref[:, cs], k_ref[:, cs]) * (C_DH ** -0.5)
        m = jnp.max(s, axis=-1, keepdims=True)
        pr = jnp.exp(s - m)
        l = jnp.sum(pr, axis=-1, keepdims=True)
        o = jnp.dot(pr.astype(BF16), v_ref[:, cs], preferred_element_type=F32) / l
        o_ref[:, cs] = (o * jax.nn.silu(gate_ref[:, cs].astype(F32))).astype(o_ref.dtype)


def _xattn(p, kv, *, batch, seq, tq):
    t = batch * seq
    nq = seq // tq
    return pl.pallas_call(
        _xattn_kernel,
        grid=(batch, nq),
        in_specs=[
            pl.BlockSpec((tq, 1024), lambda b, i: (b * nq + i, COL_QC_1024)),
            pl.BlockSpec((tq, 1024), lambda b, i: (b * nq + i, COL_GC_1024)),
            pl.BlockSpec((N_MEM, 1024), lambda b, i: (b, 0)),
            pl.BlockSpec((N_MEM, 1024), lambda b, i: (b, 1)),
        ],
        out_specs=pl.BlockSpec((tq, 1024), lambda b, i: (b * nq + i, 0)),
        out_shape=jax.ShapeDtypeStruct((t, BRANCH_WIDTH), BF16),
        compiler_params=pltpu.CompilerParams(
            dimension_semantics=("arbitrary", "arbitrary"),
            vmem_limit_bytes=VMEM_LIMIT_BYTES),
        name="mem_xattn",
    )(p, p, kv, kv)


def _merge_kernel(oa_ref, ob_ref, oc_ref, g0_ref, g1_ref, g2_ref, x_ref, wb_ref, wo_ref,
                  fg_ref, o_ref, *, final):
    y = None
    for o_r, g_r, j in ((oa_ref, g0_ref, 0), (ob_ref, g1_ref, 1), (oc_ref, g2_ref, 2)):
        term = jax.nn.sigmoid(g_r[...].astype(F32)) * jnp.dot(
            o_r[...], wb_ref[j], preferred_element_type=F32)
        y = term if y is None else y + term
    xn = x_ref[...] + jnp.dot(y.astype(BF16), wo_ref[...], preferred_element_type=F32)
    if final:
        ms = jnp.mean(xn * xn, axis=-1, keepdims=True)
        xn = xn * lax.rsqrt(ms + EPS) * fg_ref[...]
    o_ref[...] = xn


def _merge(oa, ob, oc, p, x, wb, wo, final_g, *, layer, tm, final):
    t = x.shape[0]
    row = lambda i: (i, 0)
    const2 = lambda i: (0, 0)
    kernel = functools.partial(_merge_kernel, final=final)
    return pl.pallas_call(
        kernel,
        grid=(t // tm,),
        in_specs=[
            pl.BlockSpec((tm, BRANCH_WIDTH), row),
            pl.BlockSpec((tm, BRANCH_WIDTH), row),
            pl.BlockSpec((tm, BRANCH_WIDTH), row),
            pl.BlockSpec((tm, D_MODEL), lambda i: (i, COL_GL_2048)),
            pl.BlockSpec((tm, D_MODEL), lambda i: (i, COL_GL_2048 + 1)),
            pl.BlockSpec((tm, D_MODEL), lambda i: (i, COL_GL_2048 + 2)),
            pl.BlockSpec((tm, D_MODEL), row),
            pl.BlockSpec((None, N_BRANCH, BRANCH_WIDTH, D_MODEL), lambda i: (layer, 0, 0, 0),
                         pipeline_mode=pl.Buffered(1)),
            pl.BlockSpec((None, D_MODEL, D_MODEL), lambda i: (layer, 0, 0),
                         pipeline_mode=pl.Buffered(1)),
            pl.BlockSpec((1, D_MODEL), const2),
        ],
        out_specs=pl.BlockSpec((tm, D_MODEL), row),
        out_shape=jax.ShapeDtypeStruct((t, D_MODEL), F32),
        compiler_params=pltpu.CompilerParams(
            dimension_semantics=("arbitrary",),
            vmem_limit_bytes=VMEM_LIMIT_BYTES),
        name="merge_out",
    )(oa, ob, oc, p, p, p, x, wb, wo, final_g.reshape(1, D_MODEL))


def kernel(x, mem, norm_g, w_in, diff_lambda, diff_subln_g, hgrn_lb_raw, hgrn_norm_g,
           mem_norm_g, w_kv_mem, w_branch, w_out, final_norm_g):
    batch, seq, d = x.shape
    depth = w_in.shape[0]
    xt = x.reshape(batch * seq, d)
    memt = mem.reshape(batch * N_MEM, d)
    w_in_b = w_in.astype(BF16)
    w_kv_b = w_kv_mem.astype(BF16)
    w_br_b = w_branch.astype(BF16)
    w_out_b = w_out.astype(BF16)
    lb_raw = hgrn_lb_raw.astype(F32)
    for l in range(depth):
        lam_init = 0.8 - 0.6 * math.exp(-0.3 * l)
        p = _norm_proj(xt, norm_g[l], w_in_b, layer=l, tm=1024, tn=1024)
        kv = _norm_proj(memt, mem_norm_g[l], w_kv_b, layer=l, tm=1024, tn=1024)
        oa = _diff_attn(p, diff_lambda[l], diff_subln_g[l], batch=batch, seq=seq,
                        lam_init=lam_init, tq=512, tk=256)
        ob = _hgrn(p, lb_raw, hgrn_norm_g[l], batch=batch, seq=seq, layer=l, rows=256)
        oc = _xattn(p, kv, batch=batch, seq=seq, tq=512)
        xt = _merge(oa, ob, oc, p, xt, w_br_b, w_out_b, final_norm_g,
                    layer=l, tm=256, final=(l == depth - 1))
    return xt.reshape(batch, seq, d)
```

```python
import functools
import math

import jax
import jax.numpy as jnp
import numpy as np
from jax import lax
from jax.experimental import pallas as pl
from jax.experimental.pallas import tpu as pltpu

F32 = jnp.float32
BF16 = jnp.bfloat16

D_MODEL = 2048
N_MEM = 256
A_HEADS = 8
A_DQK = 64
A_DV = 128
B_HEADS = 8
B_DK = 128
C_HEADS = 4
C_DH = 256
BRANCH_WIDTH = 1024
N_BRANCH = 3
EPS = 1e-6

COL_QA, COL_KA, COL_VA, COL_GA = 0, 8, 16, 24
COL_FB, COL_IB, COL_QB, COL_GB = 32, 40, 48, 56
COL_QC_1024, COL_GC_1024 = 8, 9
COL_GL_2048 = 5

VMEM_LIMIT_BYTES = 56 * 1024 * 1024

BF16_SUBLANES = 16
CHUNK = 64
MAX_CHUNK_DECAY = 120.0
SUB = 16


def _nt_dot(a, b):
    return lax.dot_general(a, b, (((1,), (1,)), ((), ())), preferred_element_type=F32)


def _tn_dot(a, b):
    return lax.dot_general(a, b, (((0,), (0,)), ((), ())), preferred_element_type=F32)


def _norm_proj_kernel(x_ref, g_ref, w_ref, o_ref, h_ref):
    @pl.when(pl.program_id(1) == 0)
    def _():
        x = x_ref[...]
        ms = jnp.mean(x * x, axis=-1, keepdims=True)
        h_ref[...] = (x * lax.rsqrt(ms + EPS) * g_ref[...]).astype(BF16)

    o_ref[...] = jnp.dot(h_ref[...], w_ref[...].astype(BF16),
                         preferred_element_type=F32).astype(o_ref.dtype)


def _norm_proj(x, g, w, *, layer, tm, tn):
    m, k = x.shape
    n = w.shape[2]
    return pl.pallas_call(
        _norm_proj_kernel,
        grid=(m // tm, n // tn),
        in_specs=[
            pl.BlockSpec((tm, k), lambda i, j: (i, 0)),
            pl.BlockSpec((1, k), lambda i, j: (0, 0)),
            pl.BlockSpec((None, k, tn), lambda i, j: (layer, 0, j)),
        ],
        out_specs=pl.BlockSpec((tm, tn), lambda i, j: (i, j)),
        out_shape=jax.ShapeDtypeStruct((m, n), BF16),
        scratch_shapes=[pltpu.VMEM((tm, k), BF16)],
        compiler_params=pltpu.CompilerParams(
            dimension_semantics=("arbitrary", "arbitrary"),
            vmem_limit_bytes=VMEM_LIMIT_BYTES),
        name="norm_proj",
    )(x, g.reshape(1, k), w)


def _attn_schedule(nq, nd):
    chains = []
    for c in range(nq // 2):
        own = (c, nq - 1 - c)
        steps = [(i, i * nd + d, s) for s, i in enumerate(own) for d in range(nd)]
        steps += [(i, j, s) for s, i in enumerate(own) for j in range(i * nd)]
        chains.append(steps)
    assert len({len(ch) for ch in chains}) == 1
    return np.asarray(chains, np.int32).transpose(1, 0, 2)


def _diff_attn_kernel(tab_ref, q_ref, k_ref, v_ref, gate_ref, lam_ref, g_ref, o_ref,
                      vt_ref, qt0_ref, qt1_ref, s0_ref, s1_ref, m_ref, acc_ref,
                      *, tq, tk, seq, lam_init):
    nq, nd = seq // tq, tq // tk
    n_chains = nq // 2
    qt_refs = (qt0_ref, qt1_ref)

    for c in range(seq // tk):
        vt_ref[:A_DV, c * tk:(c + 1) * tk] = v_ref[c * tk:(c + 1) * tk, :].T
    vt_ref[A_DV:, :] = jnp.ones((vt_ref.shape[0] - A_DV, seq), BF16)

    qscale = math.log2(math.e) * A_DQK ** -0.5
    sub = lax.broadcasted_iota(jnp.int32, (2 * A_DQK, tq), 0)
    for c in range(nq):
        qt = (q_ref[c * tq:(c + 1) * tq, :].astype(F32) * qscale).astype(BF16).T
        zero = jnp.zeros_like(qt)
        qt0_ref[:, c * tq:(c + 1) * tq] = jnp.where(sub < A_DQK, qt, zero)
        qt1_ref[:, c * tq:(c + 1) * tq] = jnp.where(sub >= A_DQK, qt, zero)

    m_ref[...] = jnp.full(m_ref.shape, -jnp.inf, F32)
    acc_ref[...] = jnp.zeros(acc_ref.shape, F32)

    s_refs = (s0_ref, s1_ref)

    def scores(t, par, masked):
        if masked:
            r_minus_c = (lax.broadcasted_iota(jnp.int32, (tk, tq), 0)
                         - lax.broadcasted_iota(jnp.int32, (tk, tq), 1))
        for c in range(n_chains):
            q0 = pl.multiple_of(tab_ref[t, c, 0] * tq, tq)
            k0 = pl.multiple_of(tab_ref[t, c, 1] * tk, tk)
            kb = k_ref[pl.ds(k0, tk), :]
            if masked:
                keep = r_minus_c <= q0 - k0
            for mp in range(2):
                s = jnp.dot(kb, qt_refs[mp][:, pl.ds(q0, tq)],
                            preferred_element_type=F32)
                if masked:
                    s = jnp.where(keep, s, -jnp.inf)
                s_refs[par][c, mp] = s

    def update(t, par):
        for c in range(n_chains):
            own = tab_ref[t, c, 2]
            k0 = pl.multiple_of(tab_ref[t, c, 1] * tk, tk)
            vtb = vt_ref[:, pl.ds(k0, tk)]
            for mp in range(2):
                s = s_refs[par][c, mp]
                m_old = m_ref[c, own, mp]
                m_new = jnp.maximum(m_old, jnp.max(s, axis=0, keepdims=True))
                p = jnp.exp2(s - m_new).astype(BF16)
                pv = jnp.dot(vtb, p, preferred_element_type=F32)
                acc_ref[c, own, mp] = jnp.exp2(m_old - m_new) * acc_ref[c, own, mp] + pv
                m_ref[c, own, mp] = m_new

    n_masked = 2 * nd
    n_trips = tab_ref.shape[0]
    assert n_masked % 2 == 0 and n_trips % 2 == 0
    scores(0, 0, True)
    for t in range(n_masked - 1):
        scores(t + 1, (t + 1) % 2, True)
        update(t, t % 2)

    def two_trips(u, carry):
        for v in range(2):
            t = n_masked - 1 + 2 * u + v
            par = (n_masked - 1 + v) % 2
            scores(t + 1, 1 - par, False)
            update(t, par)
        return carry

    lax.fori_loop(0, (n_trips - n_masked) // 2, two_trips, 0)
    update(n_trips - 1, (n_trips - 1) % 2)

    lp = lam_ref[...]
    lam = (jnp.exp(jnp.sum(lp[0:1] * lp[1:2], axis=-1, keepdims=True))
           - jnp.exp(jnp.sum(lp[2:3] * lp[3:4], axis=-1, keepdims=True)) + lam_init)
    for c in range(n_chains):
        for own, qb in enumerate((c, nq - 1 - c)):
            o1 = acc_ref[c, own, 0, :A_DV] / acc_ref[c, own, 0, A_DV:A_DV + 1]
            o2 = acc_ref[c, own, 1, :A_DV] / acc_ref[c, own, 1, A_DV:A_DV + 1]
            ot = o1 - lam * o2
            ms = jnp.mean(ot * ot, axis=0, keepdims=True)
            ot = ot * lax.rsqrt(ms + EPS) * g_ref[...] * (1.0 - lam_init)
            rows = slice(qb * tq, (qb + 1) * tq)
            o_ref[rows, :] = (ot.T * jax.nn.silu(gate_ref[rows, :].astype(F32))
                              ).astype(o_ref.dtype)


def _diff_attn(p, lam_p, subln_g, *, batch, seq, lam_init, tq, tk):
    t = batch * seq
    nq, nd = seq // tq, tq // tk
    assert nq % 2 == 0
    table = _attn_schedule(nq, nd)
    kernel = functools.partial(_diff_attn_kernel, tq=tq, tk=tk, seq=seq, lam_init=lam_init)
    acc_rows = A_DV + BF16_SUBLANES
    return pl.pallas_call(
        kernel,
        grid=(batch, A_HEADS),
        in_specs=[
            pl.BlockSpec(memory_space=pltpu.SMEM),
            pl.BlockSpec((seq, 128), lambda b, h: (b, COL_QA + h)),
            pl.BlockSpec((seq, 128), lambda b, h: (b, COL_KA + h)),
            pl.BlockSpec((seq, 128), lambda b, h: (b, COL_VA + h)),
            pl.BlockSpec((seq, 128), lambda b, h: (b, COL_GA + h)),
            pl.BlockSpec((4, A_DQK), lambda b, h: (0, 0)),
            pl.BlockSpec((A_DV, 1), lambda b, h: (0, 0)),
        ],
        out_specs=pl.BlockSpec((seq, 128), lambda b, h: (b, h)),
        out_shape=jax.ShapeDtypeStruct((t, BRANCH_WIDTH), BF16),
        scratch_shapes=[
            pltpu.VMEM((acc_rows, seq), BF16),
            pltpu.VMEM((2 * A_DQK, seq), BF16),
            pltpu.VMEM((2 * A_DQK, seq), BF16),
            pltpu.VMEM((nq // 2, 2, tk, tq), F32),
            pltpu.VMEM((nq // 2, 2, tk, tq), F32),
            pltpu.VMEM((nq // 2, 2, 2, 1, tq), F32),
            pltpu.VMEM((nq // 2, 2, 2, acc_rows, tq), F32),
        ],
        compiler_params=pltpu.CompilerParams(
            dimension_semantics=("arbitrary", "arbitrary"),
            vmem_limit_bytes=VMEM_LIMIT_BYTES),
        name="diff_attn",
    )(jnp.asarray(table), p, p, p, p, lam_p, subln_g.reshape(A_DV, 1))


def _block_cumsum(x, rows, blk):
    r = lax.broadcasted_iota(jnp.int32, (rows, rows), 0)
    c = lax.broadcasted_iota(jnp.int32, (rows, rows), 1)
    tri = jnp.where((r // blk == c // blk) & (c <= r), 1.0, 0.0).astype(BF16)
    hi = x.astype(BF16)
    rem = x - hi.astype(F32)
    mid = rem.astype(BF16)
    lo = (rem - mid.astype(F32)).astype(BF16)
    return (jnp.dot(tri, hi, preferred_element_type=F32)
            + jnp.dot(tri, mid, preferred_element_type=F32)
            + jnp.dot(tri, lo, preferred_element_type=F32))


def _hgrn_kernel(f_ref, i_ref, q_ref, gate_ref, lbraw_ref, ng_ref, o_ref,
                 st_ref, b_s, q_s, lf_s, k_s, o_s, *, layer, rows):
    @pl.when(pl.program_id(1) == 0)
    def _():
        st_ref[...] = jnp.zeros(st_ref.shape, F32)

    raw = lbraw_ref[...]
    e = jnp.exp(raw - jnp.max(raw, axis=0, keepdims=True))
    w = e / jnp.sum(e, axis=0, keepdims=True)
    lb = jnp.zeros((1, raw.shape[1]), F32)
    for j in range(1, layer + 1):
        lb = lb + w[j:j + 1]

    z = f_ref[...].astype(F32)
    e = jnp.exp(-jnp.abs(z))
    inv = 1.0 / (1.0 + e)
    log_sig = jnp.minimum(z, 0.0) + jnp.log(inv)
    k_s[...] = (1.0 - lb) * (jnp.where(z >= 0.0, e, 1.0) * inv)
    a = jnp.log(lb)
    c = jnp.log1p(-lb) + log_sig
    logf = jnp.maximum(a, c) + jnp.log(1.0 + jnp.exp(-jnp.abs(a - c)))
    qf = q_ref[...].astype(F32)
    q_s[...] = qf / (1.0 + jnp.exp(-qf))
    lf_s[...] = logf

    b_s[...] = _block_cumsum(logf, rows, CHUNK)
    totals = jnp.concatenate(
        [b_s[c * CHUNK + CHUNK - 1:(c + 1) * CHUNK, :] for c in range(rows // CHUNK)], axis=0)
    fast = jnp.max(-totals) <= MAX_CHUNK_DECAY

    @pl.when(fast)
    def _():
        t_i = lax.broadcasted_iota(jnp.int32, (CHUNK, CHUNK), 0)
        s_i = lax.broadcasted_iota(jnp.int32, (CHUNK, CHUNK), 1)
        causal = s_i <= t_i
        for c in range(rows // CHUNK):
            rs = slice(c * CHUNK, (c + 1) * CHUNK)
            for h in range(B_HEADS):
                cs = slice(h * B_DK, (h + 1) * B_DK)
                bn, qn, kn, vn = b_s[rs, cs], q_s[rs, cs], k_s[rs, cs], i_ref[rs, cs]
                bl = bn[CHUNK - 1:CHUNK]
                half = 0.5 * bl
                qp = (qn * jnp.exp(bn - half)).astype(BF16)
                kp = (kn * jnp.exp(half - bn)).astype(BF16)
                a = jnp.where(causal, _nt_dot(qp, kp), 0.0).astype(BF16)
                st = st_ref[h]
                qe = (qn * jnp.exp(bn)).astype(BF16)
                o_s[rs, cs] = (jnp.dot(a, vn, preferred_element_type=F32)
                               + _nt_dot(qe, st.astype(BF16)))
                kd = (kn * jnp.exp(bl - bn)).astype(BF16)
                st_ref[h] = st * jnp.exp(bl) + _tn_dot(vn, kd)

    @pl.when(jnp.logical_not(fast))
    def _():
        b_s[...] = _block_cumsum(lf_s[...], rows, SUB)
        t_idx = lax.broadcasted_iota(jnp.int32, (SUB, 1), 0)

        def body(n, carry):
            r0 = pl.multiple_of(n * SUB, SUB)
            for h in range(B_HEADS):
                cs = slice(h * B_DK, (h + 1) * B_DK)
                bn = b_s[pl.ds(r0, SUB), cs]
                qn = q_s[pl.ds(r0, SUB), cs]
                kn = k_s[pl.ds(r0, SUB), cs]
                vn = i_ref[pl.ds(r0, SUB), cs]
                vf = vn.astype(F32)
                st = st_ref[h]
                acc = _nt_dot((qn * jnp.exp(bn)).astype(BF16), st.astype(BF16))
                for s in range(SUB):
                    arg = jnp.where(t_idx >= s, bn - bn[s:s + 1], -jnp.inf)
                    wts = qn * (kn[s:s + 1] * jnp.exp(arg))
                    acc = acc + jnp.sum(wts, axis=-1, keepdims=True) * vf[s:s + 1]
                o_s[pl.ds(r0, SUB), cs] = acc
                bl = bn[SUB - 1:SUB]
                kd = kn * jnp.exp(bl - bn)
                st_ref[h] = st * jnp.exp(bl) + _tn_dot(vn, kd.astype(BF16))
            return carry

        lax.fori_loop(0, rows // SUB, body, 0)

    for h in range(B_HEADS):
        cs = slice(h * B_DK, (h + 1) * B_DK)
        o = o_s[:, cs]
        ms = jnp.mean(o * o, axis=-1, keepdims=True)
        y = o * lax.rsqrt(ms + EPS) * ng_ref[...]
        o_ref[:, cs] = (y * jax.nn.silu(gate_ref[:, cs].astype(F32))).astype(o_ref.dtype)


def _hgrn(p, lb_raw, norm_g, *, batch, seq, layer, rows):
    t = batch * seq
    nr = seq // rows
    kernel = functools.partial(_hgrn_kernel, layer=layer, rows=rows)
    blk = lambda col: pl.BlockSpec((rows, 1024), lambda b, i: (b * nr + i, col // 8))
    depth = lb_raw.shape[0]
    return pl.pallas_call(
        kernel,
        grid=(batch, nr),
        in_specs=[
            blk(COL_FB), blk(COL_IB), blk(COL_QB), blk(COL_GB),
            pl.BlockSpec((depth, 1024), lambda b, i: (0, 0)),
            pl.BlockSpec((1, B_DK), lambda b, i: (0, 0)),
        ],
        out_specs=pl.BlockSpec((rows, 1024), lambda b, i: (b * nr + i, 0)),
        out_shape=jax.ShapeDtypeStruct((t, BRANCH_WIDTH), BF16),
        scratch_shapes=[
            pltpu.VMEM((B_HEADS, B_DK, B_DK), F32),
            pltpu.VMEM((rows, 1024), F32),
            pltpu.VMEM((rows, 1024), F32),
            pltpu.VMEM((rows, 1024), F32),
            pltpu.VMEM((rows, 1024), F32),
            pltpu.VMEM((rows, 1024), F32),
        ],
        compiler_params=pltpu.CompilerParams(
            dimension_semantics=("arbitrary", "arbitrary"),
            vmem_limit_bytes=VMEM_LIMIT_BYTES),
        name="hgrn2",
    )(p, p, p, p, lb_raw, norm_g.reshape(1, B_DK))


def _xattn_kernel(q_ref, gate_ref, k_ref, v_ref, o_ref):
    for h in range(C_HEADS):
        cs = slice(h * C_DH, (h + 1) * C_DH)
        s = _nt_dot(q_ref[:, cs], k_ref[:, cs]) * (C_DH ** -0.5)
        m = jnp.max(s, axis=-1, keepdims=True)
        pr = jnp.exp(s - m)
        l = jnp.sum(pr, axis=-1, keepdims=True)
        o = jnp.dot(pr.astype(BF16), v_ref[:, cs], preferred_element_type=F32) / l
        o_ref[:, cs] = (o * jax.nn.silu(gate_ref[:, cs].astype(F32))).astype(o_ref.dtype)


def _xattn(p, kv, *, batch, seq, tq):
    t = batch * seq
    nq = seq // tq
    return pl.pallas_call(
        _xattn_kernel,
        grid=(batch, nq),
        in_specs=[
            pl.BlockSpec((tq, 1024), lambda b, i: (b * nq + i, COL_QC_1024)),
            pl.BlockSpec((tq, 1024), lambda b, i: (b * nq + i, COL_GC_1024)),
            pl.BlockSpec((N_MEM, 1024), lambda b, i: (b, 0)),
            pl.BlockSpec((N_MEM, 1024), lambda b, i: (b, 1)),
        ],
        out_specs=pl.BlockSpec((tq, 1024), lambda b, i: (b * nq + i, 0)),
        out_shape=jax.ShapeDtypeStruct((t, BRANCH_WIDTH), BF16),
        compiler_params=pltpu.CompilerParams(
            dimension_semantics=("arbitrary", "arbitrary"),
            vmem_limit_bytes=VMEM_LIMIT_BYTES),
        name="mem_xattn",
    )(p, p, kv, kv)


def _merge_kernel(oa_ref, ob_ref, oc_ref, g0_ref, g1_ref, g2_ref, x_ref, wb_ref, wo_ref,
                  fg_ref, o_ref, *, final):
    y = None
    for o_r, g_r, j in ((oa_ref, g0_ref, 0), (ob_ref, g1_ref, 1), (oc_ref, g2_ref, 2)):
        term = jax.nn.sigmoid(g_r[...].astype(F32)) * jnp.dot(
            o_r[...], wb_ref[j], preferred_element_type=F32)
        y = term if y is None else y + term
    xn = x_ref[...] + jnp.dot(y.astype(BF16), wo_ref[...], preferred_element_type=F32)
    if final:
        ms = jnp.mean(xn * xn, axis=-1, keepdims=True)
        xn = xn * lax.rsqrt(ms + EPS) * fg_ref[...]
    o_ref[...] = xn


def _merge(oa, ob, oc, p, x, wb, wo, final_g, *, layer, tm, final):
    t = x.shape[0]
    row = lambda i: (i, 0)
    const2 = lambda i: (0, 0)
    kernel = functools.partial(_merge_kernel, final=final)
    return pl.pallas_call(
        kernel,
        grid=(t // tm,),
        in_specs=[
            pl.BlockSpec((tm, BRANCH_WIDTH), row),
            pl.BlockSpec((tm, BRANCH_WIDTH), row),
            pl.BlockSpec((tm, BRANCH_WIDTH), row),
            pl.BlockSpec((tm, D_MODEL), lambda i: (i, COL_GL_2048)),
            pl.BlockSpec((tm, D_MODEL), lambda i: (i, COL_GL_2048 + 1)),
            pl.BlockSpec((tm, D_MODEL), lambda i: (i, COL_GL_2048 + 2)),
            pl.BlockSpec((tm, D_MODEL), row),
            pl.BlockSpec((None, N_BRANCH, BRANCH_WIDTH, D_MODEL), lambda i: (layer, 0, 0, 0),
                         pipeline_mode=pl.Buffered(1)),
            pl.BlockSpec((None, D_MODEL, D_MODEL), lambda i: (layer, 0, 0),
                         pipeline_mode=pl.Buffered(1)),
            pl.BlockSpec((1, D_MODEL), const2),
        ],
        out_specs=pl.BlockSpec((tm, D_MODEL), row),
        out_shape=jax.ShapeDtypeStruct((t, D_MODEL), F32),
        compiler_params=pltpu.CompilerParams(
            dimension_semantics=("arbitrary",),
            vmem_limit_bytes=VMEM_LIMIT_BYTES),
        name="merge_out",
    )(oa, ob, oc, p, p, p, x, wb, wo, final_g.reshape(1, D_MODEL))


def kernel(x, mem, norm_g, w_in, diff_lambda, diff_subln_g, hgrn_lb_raw, hgrn_norm_g,
           mem_norm_g, w_kv_mem, w_branch, w_out, final_norm_g):
    batch, seq, d = x.shape
    depth = w_in.shape[0]
    xt = x.reshape(batch * seq, d)
    memt = mem.reshape(batch * N_MEM, d)
    w_br_b = w_branch.astype(BF16)
    w_out_b = w_out.astype(BF16)
    lb_raw = hgrn_lb_raw.astype(F32)
    for l in range(depth):
        lam_init = 0.8 - 0.6 * math.exp(-0.3 * l)
        p = _norm_proj(xt, norm_g[l], w_in, layer=l, tm=1024, tn=1024)
        kv = _norm_proj(memt, mem_norm_g[l], w_kv_mem, layer=l, tm=1024, tn=1024)
        oa = _diff_attn(p, diff_lambda[l], diff_subln_g[l], batch=batch, seq=seq,
                        lam_init=lam_init, tq=512, tk=256)
        ob = _hgrn(p, lb_raw, hgrn_norm_g[l], batch=batch, seq=seq, layer=l, rows=256)
        oc = _xattn(p, kv, batch=batch, seq=seq, tq=512)
        xt = _merge(oa, ob, oc, p, xt, w_br_b, w_out_b, final_norm_g,
                    layer=l, tm=256, final=(l == depth - 1))
    return xt.reshape(batch, seq, d)
```

```python
import functools
import math

import jax
import jax.numpy as jnp
import numpy as np
from jax import lax
from jax.experimental import pallas as pl
from jax.experimental.pallas import tpu as pltpu

F32 = jnp.float32
BF16 = jnp.bfloat16

D_MODEL = 2048
N_MEM = 256
A_HEADS = 8
A_DQK = 64
A_DV = 128
B_HEADS = 8
B_DK = 128
C_HEADS = 4
C_DH = 256
BRANCH_WIDTH = 1024
N_BRANCH = 3
EPS = 1e-6

COL_QA, COL_KA, COL_VA, COL_GA = 0, 8, 16, 24
COL_FB, COL_IB, COL_QB, COL_GB = 32, 40, 48, 56
COL_QC_1024, COL_GC_1024 = 8, 9
COL_GL_2048 = 5

VMEM_LIMIT_BYTES = 56 * 1024 * 1024

BF16_SUBLANES = 16
CHUNK = 64
MAX_CHUNK_DECAY = 120.0
SUB = 16


def _nt_dot(a, b):
    return lax.dot_general(a, b, (((1,), (1,)), ((), ())), preferred_element_type=F32)


def _tn_dot(a, b):
    return lax.dot_general(a, b, (((0,), (0,)), ((), ())), preferred_element_type=F32)


def _norm_proj_kernel(x_ref, g_ref, w_ref, o_ref, h_ref):
    @pl.when(pl.program_id(1) == 0)
    def _():
        x = x_ref[...]
        ms = jnp.mean(x * x, axis=-1, keepdims=True)
        h_ref[...] = (x * lax.rsqrt(ms + EPS) * g_ref[...]).astype(BF16)

    o_ref[...] = jnp.dot(h_ref[...], w_ref[...].astype(BF16),
                         preferred_element_type=F32).astype(o_ref.dtype)


def _norm_proj(x, g, w, *, layer, tm, tn):
    m, k = x.shape
    n = w.shape[2]
    return pl.pallas_call(
        _norm_proj_kernel,
        grid=(m // tm, n // tn),
        in_specs=[
            pl.BlockSpec((tm, k), lambda i, j: (i, 0)),
            pl.BlockSpec((1, k), lambda i, j: (0, 0)),
            pl.BlockSpec((None, k, tn), lambda i, j: (layer, 0, j)),
        ],
        out_specs=pl.BlockSpec((tm, tn), lambda i, j: (i, j)),
        out_shape=jax.ShapeDtypeStruct((m, n), BF16),
        scratch_shapes=[pltpu.VMEM((tm, k), BF16)],
        compiler_params=pltpu.CompilerParams(
            dimension_semantics=("arbitrary", "arbitrary"),
            vmem_limit_bytes=VMEM_LIMIT_BYTES),
        name="norm_proj",
    )(x, g.reshape(1, k), w)


def _attn_schedule(nq, nd):
    chains = []
    for c in range(nq // 2):
        own = (c, nq - 1 - c)
        steps = [(i, i * nd + d, s) for s, i in enumerate(own) for d in range(nd)]
        steps += [(i, j, s) for s, i in enumerate(own) for j in range(i * nd)]
        chains.append(steps)
    assert len({len(ch) for ch in chains}) == 1
    return np.asarray(chains, np.int32).transpose(1, 0, 2)


def _diff_attn_kernel(tab_ref, q_ref, k_ref, v_ref, gate_ref, lam_ref, g_ref, o_ref,
                      vt_ref, qt0_ref, qt1_ref, s0_ref, s1_ref, m_ref, acc_ref,
                      *, tq, tk, seq, lam_init):
    nq, nd = seq // tq, tq // tk
    n_chains = nq // 2
    qt_refs = (qt0_ref, qt1_ref)

    for c in range(seq // tk):
        vt_ref[:A_DV, c * tk:(c + 1) * tk] = v_ref[c * tk:(c + 1) * tk, :].T
    vt_ref[A_DV:, :] = jnp.ones((vt_ref.shape[0] - A_DV, seq), BF16)

    qscale = math.log2(math.e) * A_DQK ** -0.5
    sub = lax.broadcasted_iota(jnp.int32, (2 * A_DQK, tq), 0)
    for c in range(nq):
        qt = (q_ref[c * tq:(c + 1) * tq, :].astype(F32) * qscale).astype(BF16).T
        zero = jnp.zeros_like(qt)
        qt0_ref[:, c * tq:(c + 1) * tq] = jnp.where(sub < A_DQK, qt, zero)
        qt1_ref[:, c * tq:(c + 1) * tq] = jnp.where(sub >= A_DQK, qt, zero)

    m_ref[...] = jnp.full(m_ref.shape, -jnp.inf, F32)
    acc_ref[...] = jnp.zeros(acc_ref.shape, F32)

    s_refs = (s0_ref, s1_ref)

    def scores(t, par, masked):
        if masked:
            r_minus_c = (lax.broadcasted_iota(jnp.int32, (tk, tq), 0)
                         - lax.broadcasted_iota(jnp.int32, (tk, tq), 1))
        for c in range(n_chains):
            q0 = pl.multiple_of(tab_ref[t, c, 0] * tq, tq)
            k0 = pl.multiple_of(tab_ref[t, c, 1] * tk, tk)
            kb = k_ref[pl.ds(k0, tk), :]
            if masked:
                keep = r_minus_c <= q0 - k0
            for mp in range(2):
                s = jnp.dot(kb, qt_refs[mp][:, pl.ds(q0, tq)],
                            preferred_element_type=F32)
                if masked:
                    s = jnp.where(keep, s, -jnp.inf)
                s_refs[par][c, mp] = s

    def update(t, par):
        for c in range(n_chains):
            own = tab_ref[t, c, 2]
            k0 = pl.multiple_of(tab_ref[t, c, 1] * tk, tk)
            vtb = vt_ref[:, pl.ds(k0, tk)]
            for mp in range(2):
                s = s_refs[par][c, mp]
                m_old = m_ref[c, own, mp]
                m_new = jnp.maximum(m_old, jnp.max(s, axis=0, keepdims=True))
                p = jnp.exp2(s - m_new).astype(BF16)
                pv = jnp.dot(vtb, p, preferred_element_type=F32)
                acc_ref[c, own, mp] = jnp.exp2(m_old - m_new) * acc_ref[c, own, mp] + pv
                m_ref[c, own, mp] = m_new

    n_masked = 2 * nd
    n_trips = tab_ref.shape[0]
    assert n_masked % 2 == 0 and n_trips % 2 == 0
    scores(0, 0, True)
    for t in range(n_masked - 1):
        scores(t + 1, (t + 1) % 2, True)
        update(t, t % 2)

    def two_trips(u, carry):
        for v in range(2):
            t = n_masked - 1 + 2 * u + v
            par = (n_masked - 1 + v) % 2
            scores(t + 1, 1 - par, False)
            update(t, par)
        return carry

    lax.fori_loop(0, (n_trips - n_masked) // 2, two_trips, 0)
    update(n_trips - 1, (n_trips - 1) % 2)

    lp = lam_ref[...]
    lam = (jnp.exp(jnp.sum(lp[0:1] * lp[1:2], axis=-1, keepdims=True))
           - jnp.exp(jnp.sum(lp[2:3] * lp[3:4], axis=-1, keepdims=True)) + lam_init)
    for c in range(n_chains):
        for own, qb in enumerate((c, nq - 1 - c)):
            inv_l1 = 1.0 / acc_ref[c, own, 0, A_DV:A_DV + 1]
            inv_l2 = lam / acc_ref[c, own, 1, A_DV:A_DV + 1]
            ot = acc_ref[c, own, 0, :A_DV] * inv_l1 - acc_ref[c, own, 1, :A_DV] * inv_l2
            ms = jnp.mean(ot * ot, axis=0, keepdims=True)
            ot = ot * lax.rsqrt(ms + EPS) * g_ref[...] * (1.0 - lam_init)
            rows = slice(qb * tq, (qb + 1) * tq)
            o_ref[rows, :] = (ot.T * jax.nn.silu(gate_ref[rows, :].astype(F32))
                              ).astype(o_ref.dtype)


def _diff_attn(p, lam_p, subln_g, *, batch, seq, lam_init, tq, tk):
    t = batch * seq
    nq, nd = seq // tq, tq // tk
    assert nq % 2 == 0
    table = _attn_schedule(nq, nd)
    kernel = functools.partial(_diff_attn_kernel, tq=tq, tk=tk, seq=seq, lam_init=lam_init)
    acc_rows = A_DV + BF16_SUBLANES
    return pl.pallas_call(
        kernel,
        grid=(batch, A_HEADS),
        in_specs=[
            pl.BlockSpec(memory_space=pltpu.SMEM),
            pl.BlockSpec((seq, 128), lambda b, h: (b, COL_QA + h)),
            pl.BlockSpec((seq, 128), lambda b, h: (b, COL_KA + h)),
            pl.BlockSpec((seq, 128), lambda b, h: (b, COL_VA + h)),
            pl.BlockSpec((seq, 128), lambda b, h: (b, COL_GA + h)),
            pl.BlockSpec((4, A_DQK), lambda b, h: (0, 0)),
            pl.BlockSpec((A_DV, 1), lambda b, h: (0, 0)),
        ],
        out_specs=pl.BlockSpec((seq, 128), lambda b, h: (b, h)),
        out_shape=jax.ShapeDtypeStruct((t, BRANCH_WIDTH), BF16),
        scratch_shapes=[
            pltpu.VMEM((acc_rows, seq), BF16),
            pltpu.VMEM((2 * A_DQK, seq), BF16),
            pltpu.VMEM((2 * A_DQK, seq), BF16),
            pltpu.VMEM((nq // 2, 2, tk, tq), F32),
            pltpu.VMEM((nq // 2, 2, tk, tq), F32),
            pltpu.VMEM((nq // 2, 2, 2, 1, tq), F32),
            pltpu.VMEM((nq // 2, 2, 2, acc_rows, tq), F32),
        ],
        compiler_params=pltpu.CompilerParams(
            dimension_semantics=("arbitrary", "arbitrary"),
            vmem_limit_bytes=VMEM_LIMIT_BYTES),
        name="diff_attn",
    )(jnp.asarray(table), p, p, p, p, lam_p, subln_g.reshape(A_DV, 1))


def _block_cumsum(x, rows, blk):
    r = lax.broadcasted_iota(jnp.int32, (rows, rows), 0)
    c = lax.broadcasted_iota(jnp.int32, (rows, rows), 1)
    tri = jnp.where((r // blk == c // blk) & (c <= r), 1.0, 0.0).astype(BF16)
    hi = x.astype(BF16)
    rem = x - hi.astype(F32)
    mid = rem.astype(BF16)
    lo = (rem - mid.astype(F32)).astype(BF16)
    return (jnp.dot(tri, hi, preferred_element_type=F32)
            + jnp.dot(tri, mid, preferred_element_type=F32)
            + jnp.dot(tri, lo, preferred_element_type=F32))


def _hgrn_kernel(f_ref, i_ref, q_ref, gate_ref, lbraw_ref, ng_ref, o_ref,
                 st_ref, b_s, q_s, lf_s, k_s, o_s, *, layer, rows):
    @pl.when(pl.program_id(1) == 0)
    def _():
        st_ref[...] = jnp.zeros(st_ref.shape, F32)

    raw = lbraw_ref[...]
    e = jnp.exp(raw - jnp.max(raw, axis=0, keepdims=True))
    w = e / jnp.sum(e, axis=0, keepdims=True)
    lb = jnp.zeros((1, raw.shape[1]), F32)
    for j in range(1, layer + 1):
        lb = lb + w[j:j + 1]

    z = f_ref[...].astype(F32)
    e = jnp.exp(-jnp.abs(z))
    inv = 1.0 / (1.0 + e)
    log_sig = jnp.minimum(z, 0.0) + jnp.log(inv)
    k_s[...] = (1.0 - lb) * (jnp.where(z >= 0.0, e, 1.0) * inv)
    a = jnp.log(lb)
    c = jnp.log1p(-lb) + log_sig
    logf = jnp.maximum(a, c) + jnp.log(1.0 + jnp.exp(-jnp.abs(a - c)))
    qf = q_ref[...].astype(F32)
    q_s[...] = qf / (1.0 + jnp.exp(-qf))
    lf_s[...] = logf

    b_s[...] = _block_cumsum(logf, rows, CHUNK)
    totals = jnp.concatenate(
        [b_s[c * CHUNK + CHUNK - 1:(c + 1) * CHUNK, :] for c in range(rows // CHUNK)], axis=0)
    fast = jnp.max(-totals) <= MAX_CHUNK_DECAY

    @pl.when(fast)
    def _():
        t_i = lax.broadcasted_iota(jnp.int32, (CHUNK, CHUNK), 0)
        s_i = lax.broadcasted_iota(jnp.int32, (CHUNK, CHUNK), 1)
        causal = s_i <= t_i
        for c in range(rows // CHUNK):
            rs = slice(c * CHUNK, (c + 1) * CHUNK)
            weights, inter = [], []
            for h in range(B_HEADS):
                cs = slice(h * B_DK, (h + 1) * B_DK)
                bn, qn, kn, vn = b_s[rs, cs], q_s[rs, cs], k_s[rs, cs], i_ref[rs, cs]
                bl = bn[CHUNK - 1:CHUNK]
                half = 0.5 * bl
                qp = (qn * jnp.exp(bn - half)).astype(BF16)
                kp = (kn * jnp.exp(half - bn)).astype(BF16)
                weights.append(_nt_dot(qp, kp))
                st = st_ref[h]
                qe = (qn * jnp.exp(bn)).astype(BF16)
                inter.append(_nt_dot(qe, st.astype(BF16)))
                kd = (kn * jnp.exp(bl - bn)).astype(BF16)
                st_ref[h] = st * jnp.exp(bl) + _tn_dot(vn, kd)
            for h in range(B_HEADS):
                cs = slice(h * B_DK, (h + 1) * B_DK)
                a = jnp.where(causal, weights[h], 0.0).astype(BF16)
                o_s[rs, cs] = inter[h] + jnp.dot(a, i_ref[rs, cs], preferred_element_type=F32)

    @pl.when(jnp.logical_not(fast))
    def _():
        b_s[...] = _block_cumsum(lf_s[...], rows, SUB)
        t_idx = lax.broadcasted_iota(jnp.int32, (SUB, 1), 0)

        def body(n, carry):
            r0 = pl.multiple_of(n * SUB, SUB)
            for h in range(B_HEADS):
                cs = slice(h * B_DK, (h + 1) * B_DK)
                bn = b_s[pl.ds(r0, SUB), cs]
                qn = q_s[pl.ds(r0, SUB), cs]
                kn = k_s[pl.ds(r0, SUB), cs]
                vn = i_ref[pl.ds(r0, SUB), cs]
                vf = vn.astype(F32)
                st = st_ref[h]
                acc = _nt_dot((qn * jnp.exp(bn)).astype(BF16), st.astype(BF16))
                for s in range(SUB):
                    arg = jnp.where(t_idx >= s, bn - bn[s:s + 1], -jnp.inf)
                    wts = qn * (kn[s:s + 1] * jnp.exp(arg))
                    acc = acc + jnp.sum(wts, axis=-1, keepdims=True) * vf[s:s + 1]
                o_s[pl.ds(r0, SUB), cs] = acc
                bl = bn[SUB - 1:SUB]
                kd = kn * jnp.exp(bl - bn)
                st_ref[h] = st * jnp.exp(bl) + _tn_dot(vn, kd.astype(BF16))
            return carry

        lax.fori_loop(0, rows // SUB, body, 0)

    for h in range(B_HEADS):
        cs = slice(h * B_DK, (h + 1) * B_DK)
        o = o_s[:, cs]
        ms = jnp.mean(o * o, axis=-1, keepdims=True)
        y = o * lax.rsqrt(ms + EPS) * ng_ref[...]
        o_ref[:, cs] = (y * jax.nn.silu(gate_ref[:, cs].astype(F32))).astype(o_ref.dtype)


def _hgrn(p, lb_raw, norm_g, *, batch, seq, layer, rows):
    t = batch * seq
    nr = seq // rows
    kernel = functools.partial(_hgrn_kernel, layer=layer, rows=rows)
    blk = lambda col: pl.BlockSpec((rows, 1024), lambda b, i: (b * nr + i, col // 8))
    depth = lb_raw.shape[0]
    return pl.pallas_call(
        kernel,
        grid=(batch, nr),
        in_specs=[
            blk(COL_FB), blk(COL_IB), blk(COL_QB), blk(COL_GB),
            pl.BlockSpec((depth, 1024), lambda b, i: (0, 0)),
            pl.BlockSpec((1, B_DK), lambda b, i: (0, 0)),
        ],
        out_specs=pl.BlockSpec((rows, 1024), lambda b, i: (b * nr + i, 0)),
        out_shape=jax.ShapeDtypeStruct((t, BRANCH_WIDTH), BF16),
        scratch_shapes=[
            pltpu.VMEM((B_HEADS, B_DK, B_DK), F32),
            pltpu.VMEM((rows, 1024), F32),
            pltpu.VMEM((rows, 1024), F32),
            pltpu.VMEM((rows, 1024), F32),
            pltpu.VMEM((rows, 1024), F32),
            pltpu.VMEM((rows, 1024), F32),
        ],
        compiler_params=pltpu.CompilerParams(
            dimension_semantics=("arbitrary", "arbitrary"),
            vmem_limit_bytes=VMEM_LIMIT_BYTES),
        name="hgrn2",
    )(p, p, p, p, lb_raw, norm_g.reshape(1, B_DK))


def _xattn_kernel(q_ref, gate_ref, k_ref, v_ref, o_ref):
    heads = [slice(h * C_DH, (h + 1) * C_DH) for h in range(C_HEADS)]
    scores = [_nt_dot(q_ref[:, cs], k_ref[:, cs]) for cs in heads]
    for cs, s in zip(heads, scores):
        s = s * (C_DH ** -0.5)
        m = jnp.max(s, axis=-1, keepdims=True)
        pr = jnp.exp(s - m)
        l = jnp.sum(pr, axis=-1, keepdims=True)
        o = jnp.dot(pr.astype(BF16), v_ref[:, cs], preferred_element_type=F32) / l
        o_ref[:, cs] = (o * jax.nn.silu(gate_ref[:, cs].astype(F32))).astype(o_ref.dtype)


def _xattn(p, kv, *, batch, seq, tq):
    t = batch * seq
    nq = seq // tq
    return pl.pallas_call(
        _xattn_kernel,
        grid=(batch, nq),
        in_specs=[
            pl.BlockSpec((tq, 1024), lambda b, i: (b * nq + i, COL_QC_1024)),
            pl.BlockSpec((tq, 1024), lambda b, i: (b * nq + i, COL_GC_1024)),
            pl.BlockSpec((N_MEM, 1024), lambda b, i: (b, 0)),
            pl.BlockSpec((N_MEM, 1024), lambda b, i: (b, 1)),
        ],
        out_specs=pl.BlockSpec((tq, 1024), lambda b, i: (b * nq + i, 0)),
        out_shape=jax.ShapeDtypeStruct((t, BRANCH_WIDTH), BF16),
        compiler_params=pltpu.CompilerParams(
            dimension_semantics=("arbitrary", "arbitrary"),
            vmem_limit_bytes=VMEM_LIMIT_BYTES),
        name="mem_xattn",
    )(p, p, kv, kv)


def _merge_kernel(oa_ref, ob_ref, oc_ref, g0_ref, g1_ref, g2_ref, x_ref, wb_ref, wo_ref,
                  fg_ref, o_ref, *, final):
    y = None
    for o_r, g_r, j in ((oa_ref, g0_ref, 0), (ob_ref, g1_ref, 1), (oc_ref, g2_ref, 2)):
        term = jax.nn.sigmoid(g_r[...].astype(F32)) * jnp.dot(
            o_r[...], wb_ref[j], preferred_element_type=F32)
        y = term if y is None else y + term
    xn = x_ref[...] + jnp.dot(y.astype(BF16), wo_ref[...], preferred_element_type=F32)
    if final:
        ms = jnp.mean(xn * xn, axis=-1, keepdims=True)
        xn = xn * lax.rsqrt(ms + EPS) * fg_ref[...]
    o_ref[...] = xn


def _merge(oa, ob, oc, p, x, wb, wo, final_g, *, layer, tm, final):
    t = x.shape[0]
    row = lambda i: (i, 0)
    const2 = lambda i: (0, 0)
    kernel = functools.partial(_merge_kernel, final=final)
    return pl.pallas_call(
        kernel,
        grid=(t // tm,),
        in_specs=[
            pl.BlockSpec((tm, BRANCH_WIDTH), row),
            pl.BlockSpec((tm, BRANCH_WIDTH), row),
            pl.BlockSpec((tm, BRANCH_WIDTH), row),
            pl.BlockSpec((tm, D_MODEL), lambda i: (i, COL_GL_2048)),
            pl.BlockSpec((tm, D_MODEL), lambda i: (i, COL_GL_2048 + 1)),
            pl.BlockSpec((tm, D_MODEL), lambda i: (i, COL_GL_2048 + 2)),
            pl.BlockSpec((tm, D_MODEL), row),
            pl.BlockSpec((None, N_BRANCH, BRANCH_WIDTH, D_MODEL), lambda i: (layer, 0, 0, 0),
                         pipeline_mode=pl.Buffered(1)),
            pl.BlockSpec((None, D_MODEL, D_MODEL), lambda i: (layer, 0, 0),
                         pipeline_mode=pl.Buffered(1)),
            pl.BlockSpec((1, D_MODEL), const2),
        ],
        out_specs=pl.BlockSpec((tm, D_MODEL), row),
        out_shape=jax.ShapeDtypeStruct((t, D_MODEL), F32),
        compiler_params=pltpu.CompilerParams(
            dimension_semantics=("arbitrary",),
            vmem_limit_bytes=VMEM_LIMIT_BYTES),
        name="merge_out",
    )(oa, ob, oc, p, p, p, x, wb, wo, final_g.reshape(1, D_MODEL))


def kernel(x, mem, norm_g, w_in, diff_lambda, diff_subln_g, hgrn_lb_raw, hgrn_norm_g,
           mem_norm_g, w_kv_mem, w_branch, w_out, final_norm_g):
    batch, seq, d = x.shape
    depth = w_in.shape[0]
    xt = x.reshape(batch * seq, d)
    memt = mem.reshape(batch * N_MEM, d)
    w_br_b = w_branch.astype(BF16)
    w_out_b = w_out.astype(BF16)
    lb_raw = hgrn_lb_raw.astype(F32)
    for l in range(depth):
        lam_init = 0.8 - 0.6 * math.exp(-0.3 * l)
        p = _norm_proj(xt, norm_g[l], w_in, layer=l, tm=1024, tn=1024)
        kv = _norm_proj(memt, mem_norm_g[l], w_kv_mem, layer=l, tm=1024, tn=1024)
        oa = _diff_attn(p, diff_lambda[l], diff_subln_g[l], batch=batch, seq=seq,
                        lam_init=lam_init, tq=512, tk=256)
        ob = _hgrn(p, lb_raw, hgrn_norm_g[l], batch=batch, seq=seq, layer=l, rows=256)
        oc = _xattn(p, kv, batch=batch, seq=seq, tq=512)
        xt = _merge(oa, ob, oc, p, xt, w_br_b, w_out_b, final_norm_g,
                    layer=l, tm=256, final=(l == depth - 1))
    return xt.reshape(batch, seq, d)
```

```python
import functools
import math

import jax
import jax.numpy as jnp
import numpy as np
from jax import lax
from jax.experimental import pallas as pl
from jax.experimental.pallas import tpu as pltpu

F32 = jnp.float32
BF16 = jnp.bfloat16

D_MODEL = 2048
N_MEM = 256
A_HEADS = 8
A_DQK = 64
A_DV = 128
B_HEADS = 8
B_DK = 128
C_HEADS = 4
C_DH = 256
BRANCH_WIDTH = 1024
N_BRANCH = 3
EPS = 1e-6

COL_QA, COL_KA, COL_VA, COL_GA = 0, 8, 16, 24
COL_FB, COL_IB, COL_QB, COL_GB = 32, 40, 48, 56
COL_QC_1024, COL_GC_1024 = 8, 9
COL_GL_2048 = 5

VMEM_LIMIT_BYTES = 56 * 1024 * 1024

BF16_SUBLANES = 16
CHUNK = 64
MAX_CHUNK_DECAY = 120.0
SUB = 16


def _nt_dot(a, b):
    return lax.dot_general(a, b, (((1,), (1,)), ((), ())), preferred_element_type=F32)


def _tn_dot(a, b):
    return lax.dot_general(a, b, (((0,), (0,)), ((), ())), preferred_element_type=F32)


def _norm_proj_kernel(x_ref, g_ref, w_ref, o_ref, h_ref):
    @pl.when(pl.program_id(1) == 0)
    def _():
        x = x_ref[...]
        ms = jnp.mean(x * x, axis=-1, keepdims=True)
        h_ref[...] = (x * lax.rsqrt(ms + EPS) * g_ref[...]).astype(BF16)

    o_ref[...] = jnp.dot(h_ref[...], w_ref[...].astype(BF16),
                         preferred_element_type=F32).astype(o_ref.dtype)


def _norm_proj(x, g, w, *, layer, tm, tn):
    m, k = x.shape
    n = w.shape[2]
    return pl.pallas_call(
        _norm_proj_kernel,
        grid=(m // tm, n // tn),
        in_specs=[
            pl.BlockSpec((tm, k), lambda i, j: (i, 0)),
            pl.BlockSpec((1, k), lambda i, j: (0, 0)),
            pl.BlockSpec((None, k, tn), lambda i, j: (layer, 0, j)),
        ],
        out_specs=pl.BlockSpec((tm, tn), lambda i, j: (i, j)),
        out_shape=jax.ShapeDtypeStruct((m, n), BF16),
        scratch_shapes=[pltpu.VMEM((tm, k), BF16)],
        compiler_params=pltpu.CompilerParams(
            dimension_semantics=("arbitrary", "arbitrary"),
            vmem_limit_bytes=VMEM_LIMIT_BYTES),
        name="norm_proj",
    )(x, g.reshape(1, k), w)


def _attn_schedule(nq, nd):
    chains = []
    for c in range(nq // 2):
        own = (c, nq - 1 - c)
        steps = [(i, i * nd + d, s) for s, i in enumerate(own) for d in range(nd)]
        steps += [(i, j, s) for s, i in enumerate(own) for j in range(i * nd)]
        chains.append(steps)
    assert len({len(ch) for ch in chains}) == 1
    return np.asarray(chains, np.int32).transpose(1, 0, 2)


def _diff_attn_kernel(tab_ref, q_ref, k_ref, v_ref, gate_ref, lam_ref, g_ref, o_ref,
                      vt_ref, qt0_ref, qt1_ref, s0_ref, s1_ref, m_ref, acc_ref,
                      *, tq, tk, seq, lam_init):
    nq, nd = seq // tq, tq // tk
    n_chains = nq // 2
    qt_refs = (qt0_ref, qt1_ref)

    for c in range(seq // tk):
        vt_ref[:A_DV, c * tk:(c + 1) * tk] = v_ref[c * tk:(c + 1) * tk, :].T
    vt_ref[A_DV:, :] = jnp.ones((vt_ref.shape[0] - A_DV, seq), BF16)

    qscale = math.log2(math.e) * A_DQK ** -0.5
    sub = lax.broadcasted_iota(jnp.int32, (2 * A_DQK, tq), 0)
    for c in range(nq):
        qt = (q_ref[c * tq:(c + 1) * tq, :].astype(F32) * qscale).astype(BF16).T
        zero = jnp.zeros_like(qt)
        qt0_ref[:, c * tq:(c + 1) * tq] = jnp.where(sub < A_DQK, qt, zero)
        qt1_ref[:, c * tq:(c + 1) * tq] = jnp.where(sub >= A_DQK, qt, zero)

    m_ref[...] = jnp.full(m_ref.shape, -jnp.inf, F32)
    acc_ref[...] = jnp.zeros(acc_ref.shape, F32)

    s_refs = (s0_ref, s1_ref)

    assert nd == 2
    right = slice(tk, tq)

    def scores(t, par, diag=None):
        if diag is not None:
            causal = (lax.broadcasted_iota(jnp.int32, (tk, tk), 0)
                      <= lax.broadcasted_iota(jnp.int32, (tk, tk), 1))
        for c in range(n_chains):
            q0 = pl.multiple_of(tab_ref[t, c, 0] * tq, tq)
            k0 = pl.multiple_of(tab_ref[t, c, 1] * tk, tk)
            kb = k_ref[pl.ds(k0, tk), :]
            for mp in range(2):
                if diag is None:
                    s_refs[par][c, mp] = jnp.dot(kb, qt_refs[mp][:, pl.ds(q0, tq)],
                                                 preferred_element_type=F32)
                    continue
                ql = qt_refs[mp][:, pl.ds(q0, tk)]
                qr = qt_refs[mp][:, pl.ds(pl.multiple_of(q0 + tk, tk), tk)]
                if diag == 0:
                    s_refs[par][c, mp, :, :tk] = jnp.where(
                        causal, jnp.dot(kb, ql, preferred_element_type=F32), -jnp.inf)
                    s_refs[par][c, mp, :, right] = jnp.dot(kb, qr, preferred_element_type=F32)
                else:
                    s_refs[par][c, mp, :, right] = jnp.where(
                        causal, jnp.dot(kb, qr, preferred_element_type=F32), -jnp.inf)

    def update(t, par, cols=slice(None)):
        for c in range(n_chains):
            own = tab_ref[t, c, 2]
            k0 = pl.multiple_of(tab_ref[t, c, 1] * tk, tk)
            vtb = vt_ref[:, pl.ds(k0, tk)]
            for mp in range(2):
                s = s_refs[par][c, mp, :, cols]
                m_old = m_ref[c, own, mp, :, cols]
                m_new = jnp.maximum(m_old, jnp.max(s, axis=0, keepdims=True))
                p = jnp.exp2(s - m_new).astype(BF16)
                pv = jnp.dot(vtb, p, preferred_element_type=F32)
                acc_ref[c, own, mp, :, cols] = (
                    jnp.exp2(m_old - m_new) * acc_ref[c, own, mp, :, cols] + pv)
                m_ref[c, own, mp, :, cols] = m_new

    n_diag = 2 * nd
    n_trips = tab_ref.shape[0]
    assert (n_trips - n_diag) % 2 == 0
    scores(0, 0, diag=0)
    for t in range(n_diag):
        nxt = t + 1
        scores(nxt, nxt % 2, diag=(nxt % nd if nxt < n_diag else None))
        update(t, t % 2, cols=(right if t % nd == 1 else slice(None)))

    def two_trips(u, carry):
        for v in range(2):
            t = n_diag + 2 * u + v
            par = (n_diag + v) % 2
            scores(t + 1, 1 - par)
            update(t, par)
        return carry

    lax.fori_loop(0, (n_trips - n_diag - 2) // 2, two_trips, 0)
    scores(n_trips - 1, (n_trips - 1) % 2)
    update(n_trips - 2, (n_trips - 2) % 2)
    update(n_trips - 1, (n_trips - 1) % 2)

    lp = lam_ref[...]
    lam = (jnp.exp(jnp.sum(lp[0:1] * lp[1:2], axis=-1, keepdims=True))
           - jnp.exp(jnp.sum(lp[2:3] * lp[3:4], axis=-1, keepdims=True)) + lam_init)
    for c in range(n_chains):
        for own, qb in enumerate((c, nq - 1 - c)):
            inv_l1 = 1.0 / acc_ref[c, own, 0, A_DV:A_DV + 1]
            inv_l2 = lam / acc_ref[c, own, 1, A_DV:A_DV + 1]
            ot = acc_ref[c, own, 0, :A_DV] * inv_l1 - acc_ref[c, own, 1, :A_DV] * inv_l2
            ms = jnp.mean(ot * ot, axis=0, keepdims=True)
            ot = ot * lax.rsqrt(ms + EPS) * g_ref[...] * (1.0 - lam_init)
            rows = slice(qb * tq, (qb + 1) * tq)
            o_ref[rows, :] = (ot.T * jax.nn.silu(gate_ref[rows, :].astype(F32))
                              ).astype(o_ref.dtype)


def _diff_attn(p, lam_p, subln_g, *, batch, seq, lam_init, tq, tk):
    t = batch * seq
    nq, nd = seq // tq, tq // tk
    assert nq % 2 == 0
    table = _attn_schedule(nq, nd)
    kernel = functools.partial(_diff_attn_kernel, tq=tq, tk=tk, seq=seq, lam_init=lam_init)
    acc_rows = A_DV + BF16_SUBLANES
    return pl.pallas_call(
        kernel,
        grid=(batch, A_HEADS),
        in_specs=[
            pl.BlockSpec(memory_space=pltpu.SMEM),
            pl.BlockSpec((seq, 128), lambda b, h: (b, COL_QA + h)),
            pl.BlockSpec((seq, 128), lambda b, h: (b, COL_KA + h)),
            pl.BlockSpec((seq, 128), lambda b, h: (b, COL_VA + h)),
            pl.BlockSpec((seq, 128), lambda b, h: (b, COL_GA + h)),
            pl.BlockSpec((4, A_DQK), lambda b, h: (0, 0)),
            pl.BlockSpec((A_DV, 1), lambda b, h: (0, 0)),
        ],
        out_specs=pl.BlockSpec((seq, 128), lambda b, h: (b, h)),
        out_shape=jax.ShapeDtypeStruct((t, BRANCH_WIDTH), BF16),
        scratch_shapes=[
            pltpu.VMEM((acc_rows, seq), BF16),
            pltpu.VMEM((2 * A_DQK, seq), BF16),
            pltpu.VMEM((2 * A_DQK, seq), BF16),
            pltpu.VMEM((nq // 2, 2, tk, tq), F32),
            pltpu.VMEM((nq // 2, 2, tk, tq), F32),
            pltpu.VMEM((nq // 2, 2, 2, 1, tq), F32),
            pltpu.VMEM((nq // 2, 2, 2, acc_rows, tq), F32),
        ],
        compiler_params=pltpu.CompilerParams(
            dimension_semantics=("arbitrary", "arbitrary"),
            vmem_limit_bytes=VMEM_LIMIT_BYTES),
        name="diff_attn",
    )(jnp.asarray(table), p, p, p, p, lam_p, subln_g.reshape(A_DV, 1))


def _block_cumsum(x, rows, blk):
    r = lax.broadcasted_iota(jnp.int32, (rows, rows), 0)
    c = lax.broadcasted_iota(jnp.int32, (rows, rows), 1)
    tri = jnp.where((r // blk == c // blk) & (c <= r), 1.0, 0.0).astype(BF16)
    hi = x.astype(BF16)
    rem = x - hi.astype(F32)
    mid = rem.astype(BF16)
    lo = (rem - mid.astype(F32)).astype(BF16)
    return (jnp.dot(tri, hi, preferred_element_type=F32)
            + jnp.dot(tri, mid, preferred_element_type=F32)
            + jnp.dot(tri, lo, preferred_element_type=F32))


def _hgrn_kernel(f_ref, i_ref, q_ref, gate_ref, lbraw_ref, ng_ref, o_ref,
                 st_ref, b_s, q_s, lf_s, k_s, o_s, *, layer, rows):
    @pl.when(pl.program_id(1) == 0)
    def _():
        st_ref[...] = jnp.zeros(st_ref.shape, F32)

    raw = lbraw_ref[...]
    e = jnp.exp(raw - jnp.max(raw, axis=0, keepdims=True))
    w = e / jnp.sum(e, axis=0, keepdims=True)
    lb = jnp.zeros((1, raw.shape[1]), F32)
    for j in range(1, layer + 1):
        lb = lb + w[j:j + 1]

    z = f_ref[...].astype(F32)
    e = jnp.exp(-jnp.abs(z))
    inv = 1.0 / (1.0 + e)
    log_sig = jnp.minimum(z, 0.0) + jnp.log(inv)
    k_s[...] = (1.0 - lb) * (jnp.where(z >= 0.0, e, 1.0) * inv)
    a = jnp.log(lb)
    c = jnp.log1p(-lb) + log_sig
    logf = jnp.maximum(a, c) + jnp.log(1.0 + jnp.exp(-jnp.abs(a - c)))
    qf = q_ref[...].astype(F32)
    q_s[...] = qf / (1.0 + jnp.exp(-qf))
    lf_s[...] = logf

    b_s[...] = _block_cumsum(logf, rows, CHUNK)
    totals = jnp.concatenate(
        [b_s[c * CHUNK + CHUNK - 1:(c + 1) * CHUNK, :] for c in range(rows // CHUNK)], axis=0)
    fast = jnp.max(-totals) <= MAX_CHUNK_DECAY

    @pl.when(fast)
    def _():
        t_i = lax.broadcasted_iota(jnp.int32, (CHUNK, CHUNK), 0)
        s_i = lax.broadcasted_iota(jnp.int32, (CHUNK, CHUNK), 1)
        causal = s_i <= t_i
        for c in range(rows // CHUNK):
            rs = slice(c * CHUNK, (c + 1) * CHUNK)
            weights, inter = [], []
            for h in range(B_HEADS):
                cs = slice(h * B_DK, (h + 1) * B_DK)
                bn, qn, kn, vn = b_s[rs, cs], q_s[rs, cs], k_s[rs, cs], i_ref[rs, cs]
                bl = bn[CHUNK - 1:CHUNK]
                half = 0.5 * bl
                qp = (qn * jnp.exp(bn - half)).astype(BF16)
                kp = (kn * jnp.exp(half - bn)).astype(BF16)
                weights.append(_nt_dot(qp, kp))
                st = st_ref[h]
                qe = (qn * jnp.exp(bn)).astype(BF16)
                inter.append(_nt_dot(qe, st.astype(BF16)))
                kd = (kn * jnp.exp(bl - bn)).astype(BF16)
                st_ref[h] = st * jnp.exp(bl) + _tn_dot(vn, kd)
            for h in range(B_HEADS):
                cs = slice(h * B_DK, (h + 1) * B_DK)
                a = jnp.where(causal, weights[h], 0.0).astype(BF16)
                o_s[rs, cs] = inter[h] + jnp.dot(a, i_ref[rs, cs], preferred_element_type=F32)

    @pl.when(jnp.logical_not(fast))
    def _():
        b_s[...] = _block_cumsum(lf_s[...], rows, SUB)
        t_idx = lax.broadcasted_iota(jnp.int32, (SUB, 1), 0)

        def body(n, carry):
            r0 = pl.multiple_of(n * SUB, SUB)
            for h in range(B_HEADS):
                cs = slice(h * B_DK, (h + 1) * B_DK)
                bn = b_s[pl.ds(r0, SUB), cs]
                qn = q_s[pl.ds(r0, SUB), cs]
                kn = k_s[pl.ds(r0, SUB), cs]
                vn = i_ref[pl.ds(r0, SUB), cs]
                vf = vn.astype(F32)
                st = st_ref[h]
                acc = _nt_dot((qn * jnp.exp(bn)).astype(BF16), st.astype(BF16))
                for s in range(SUB):
                    arg = jnp.where(t_idx >= s, bn - bn[s:s + 1], -jnp.inf)
                    wts = qn * (kn[s:s + 1] * jnp.exp(arg))
                    acc = acc + jnp.sum(wts, axis=-1, keepdims=True) * vf[s:s + 1]
                o_s[pl.ds(r0, SUB), cs] = acc
                bl = bn[SUB - 1:SUB]
                kd = kn * jnp.exp(bl - bn)
                st_ref[h] = st * jnp.exp(bl) + _tn_dot(vn, kd.astype(BF16))
            return carry

        lax.fori_loop(0, rows // SUB, body, 0)

    for h in range(B_HEADS):
        cs = slice(h * B_DK, (h + 1) * B_DK)
        o = o_s[:, cs]
        ms = jnp.mean(o * o, axis=-1, keepdims=True)
        y = o * lax.rsqrt(ms + EPS) * ng_ref[...]
        o_ref[:, cs] = (y * jax.nn.silu(gate_ref[:, cs].astype(F32))).astype(o_ref.dtype)


def _hgrn(p, lb_raw, norm_g, *, batch, seq, layer, rows):
    t = batch * seq
    nr = seq // rows
    kernel = functools.partial(_hgrn_kernel, layer=layer, rows=rows)
    blk = lambda col: pl.BlockSpec((rows, 1024), lambda b, i: (b * nr + i, col // 8))
    depth = lb_raw.shape[0]
    return pl.pallas_call(
        kernel,
        grid=(batch, nr),
        in_specs=[
            blk(COL_FB), blk(COL_IB), blk(COL_QB), blk(COL_GB),
            pl.BlockSpec((depth, 1024), lambda b, i: (0, 0)),
            pl.BlockSpec((1, B_DK), lambda b, i: (0, 0)),
        ],
        out_specs=pl.BlockSpec((rows, 1024), lambda b, i: (b * nr + i, 0)),
        out_shape=jax.ShapeDtypeStruct((t, BRANCH_WIDTH), BF16),
        scratch_shapes=[
            pltpu.VMEM((B_HEADS, B_DK, B_DK), F32),
            pltpu.VMEM((rows, 1024), F32),
            pltpu.VMEM((rows, 1024), F32),
            pltpu.VMEM((rows, 1024), F32),
            pltpu.VMEM((rows, 1024), F32),
            pltpu.VMEM((rows, 1024), F32),
        ],
        compiler_params=pltpu.CompilerParams(
            dimension_semantics=("arbitrary", "arbitrary"),
            vmem_limit_bytes=VMEM_LIMIT_BYTES),
        name="hgrn2",
    )(p, p, p, p, lb_raw, norm_g.reshape(1, B_DK))


def _xattn_kernel(q_ref, gate_ref, k_ref, v_ref, o_ref):
    heads = [slice(h * C_DH, (h + 1) * C_DH) for h in range(C_HEADS)]
    scores = [_nt_dot(q_ref[:, cs], k_ref[:, cs]) for cs in heads]
    for cs, s in zip(heads, scores):
        s = s * (C_DH ** -0.5)
        m = jnp.max(s, axis=-1, keepdims=True)
        pr = jnp.exp(s - m)
        l = jnp.sum(pr, axis=-1, keepdims=True)
        o = jnp.dot(pr.astype(BF16), v_ref[:, cs], preferred_element_type=F32) / l
        o_ref[:, cs] = (o * jax.nn.silu(gate_ref[:, cs].astype(F32))).astype(o_ref.dtype)


def _xattn(p, kv, *, batch, seq, tq):
    t = batch * seq
    nq = seq // tq
    return pl.pallas_call(
        _xattn_kernel,
        grid=(batch, nq),
        in_specs=[
            pl.BlockSpec((tq, 1024), lambda b, i: (b * nq + i, COL_QC_1024)),
            pl.BlockSpec((tq, 1024), lambda b, i: (b * nq + i, COL_GC_1024)),
            pl.BlockSpec((N_MEM, 1024), lambda b, i: (b, 0)),
            pl.BlockSpec((N_MEM, 1024), lambda b, i: (b, 1)),
        ],
        out_specs=pl.BlockSpec((tq, 1024), lambda b, i: (b * nq + i, 0)),
        out_shape=jax.ShapeDtypeStruct((t, BRANCH_WIDTH), BF16),
        compiler_params=pltpu.CompilerParams(
            dimension_semantics=("arbitrary", "arbitrary"),
            vmem_limit_bytes=VMEM_LIMIT_BYTES),
        name="mem_xattn",
    )(p, p, kv, kv)


def _merge_kernel(oa_ref, ob_ref, oc_ref, g0_ref, g1_ref, g2_ref, x_ref, wb_ref, wo_ref,
                  fg_ref, o_ref, *, final):
    y = None
    for o_r, g_r, j in ((oa_ref, g0_ref, 0), (ob_ref, g1_ref, 1), (oc_ref, g2_ref, 2)):
        term = jax.nn.sigmoid(g_r[...].astype(F32)) * jnp.dot(
            o_r[...], wb_ref[j], preferred_element_type=F32)
        y = term if y is None else y + term
    xn = x_ref[...] + jnp.dot(y.astype(BF16), wo_ref[...], preferred_element_type=F32)
    if final:
        ms = jnp.mean(xn * xn, axis=-1, keepdims=True)
        xn = xn * lax.rsqrt(ms + EPS) * fg_ref[...]
    o_ref[...] = xn


def _merge(oa, ob, oc, p, x, wb, wo, final_g, *, layer, tm, final):
    t = x.shape[0]
    row = lambda i: (i, 0)
    const2 = lambda i: (0, 0)
    kernel = functools.partial(_merge_kernel, final=final)
    return pl.pallas_call(
        kernel,
        grid=(t // tm,),
        in_specs=[
            pl.BlockSpec((tm, BRANCH_WIDTH), row),
            pl.BlockSpec((tm, BRANCH_WIDTH), row),
            pl.BlockSpec((tm, BRANCH_WIDTH), row),
            pl.BlockSpec((tm, D_MODEL), lambda i: (i, COL_GL_2048)),
            pl.BlockSpec((tm, D_MODEL), lambda i: (i, COL_GL_2048 + 1)),
            pl.BlockSpec((tm, D_MODEL), lambda i: (i, COL_GL_2048 + 2)),
            pl.BlockSpec((tm, D_MODEL), row),
            pl.BlockSpec((None, N_BRANCH, BRANCH_WIDTH, D_MODEL), lambda i: (layer, 0, 0, 0),
                         pipeline_mode=pl.Buffered(1)),
            pl.BlockSpec((None, D_MODEL, D_MODEL), lambda i: (layer, 0, 0),
                         pipeline_mode=pl.Buffered(1)),
            pl.BlockSpec((1, D_MODEL), const2),
        ],
        out_specs=pl.BlockSpec((tm, D_MODEL), row),
        out_shape=jax.ShapeDtypeStruct((t, D_MODEL), F32),
        compiler_params=pltpu.CompilerParams(
            dimension_semantics=("arbitrary",),
            vmem_limit_bytes=VMEM_LIMIT_BYTES),
        name="merge_out",
    )(oa, ob, oc, p, p, p, x, wb, wo, final_g.reshape(1, D_MODEL))


def kernel(x, mem, norm_g, w_in, diff_lambda, diff_subln_g, hgrn_lb_raw, hgrn_norm_g,
           mem_norm_g, w_kv_mem, w_branch, w_out, final_norm_g):
    batch, seq, d = x.shape
    depth = w_in.shape[0]
    xt = x.reshape(batch * seq, d)
    memt = mem.reshape(batch * N_MEM, d)
    w_br_b = w_branch.astype(BF16)
    w_out_b = w_out.astype(BF16)
    lb_raw = hgrn_lb_raw.astype(F32)
    for l in range(depth):
        lam_init = 0.8 - 0.6 * math.exp(-0.3 * l)
        p = _norm_proj(xt, norm_g[l], w_in, layer=l, tm=1024, tn=1024)
        kv = _norm_proj(memt, mem_norm_g[l], w_kv_mem, layer=l, tm=1024, tn=1024)
        oa = _diff_attn(p, diff_lambda[l], diff_subln_g[l], batch=batch, seq=seq,
                        lam_init=lam_init, tq=512, tk=256)
        ob = _hgrn(p, lb_raw, hgrn_norm_g[l], batch=batch, seq=seq, layer=l, rows=256)
        oc = _xattn(p, kv, batch=batch, seq=seq, tq=512)
        xt = _merge(oa, ob, oc, p, xt, w_br_b, w_out_b, final_norm_g,
                    layer=l, tm=256, final=(l == depth - 1))
    return xt.reshape(batch, seq, d)
```

```python
import functools
import math

import jax
import jax.numpy as jnp
import numpy as np
from jax import lax
from jax.experimental import pallas as pl
from jax.experimental.pallas import tpu as pltpu

F32 = jnp.float32
BF16 = jnp.bfloat16

D_MODEL = 2048
N_MEM = 256
A_HEADS = 8
A_DQK = 64
A_DV = 128
B_HEADS = 8
B_DK = 128
C_HEADS = 4
C_DH = 256
BRANCH_WIDTH = 1024
N_BRANCH = 3
EPS = 1e-6

COL_QA, COL_KA, COL_VA, COL_GA = 0, 8, 16, 24
COL_FB, COL_IB, COL_QB, COL_GB = 32, 40, 48, 56
COL_QC_1024, COL_GC_1024 = 8, 9
COL_GL_2048 = 5

VMEM_LIMIT_BYTES = 56 * 1024 * 1024

BF16_SUBLANES = 16
CHUNK = 64
MAX_CHUNK_DECAY = 120.0
SUB = 16


def _nt_dot(a, b):
    return lax.dot_general(a, b, (((1,), (1,)), ((), ())), preferred_element_type=F32)


def _tn_dot(a, b):
    return lax.dot_general(a, b, (((0,), (0,)), ((), ())), preferred_element_type=F32)


def _norm_proj_kernel(x_ref, g_ref, w_ref, *refs, emit_bf16_w, aliased_out):
    refs = refs[1:] if aliased_out else refs
    o_ref, h_ref = refs[0], refs[-1]

    @pl.when(pl.program_id(1) == 0)
    def _():
        x = x_ref[...]
        ms = jnp.mean(x * x, axis=-1, keepdims=True)
        h_ref[...] = (x * lax.rsqrt(ms + EPS) * g_ref[...]).astype(BF16)

    w = w_ref[...].astype(BF16)
    if emit_bf16_w:
        refs[1][...] = w
    o_ref[...] = jnp.dot(h_ref[...], w, preferred_element_type=F32).astype(o_ref.dtype)


def _norm_proj_call(kernel_kwargs, grid, in_specs, out_specs, out_shape, tm, k, **call_kwargs):
    return pl.pallas_call(
        functools.partial(_norm_proj_kernel, **kernel_kwargs),
        grid=grid, in_specs=in_specs, out_specs=out_specs, out_shape=out_shape,
        scratch_shapes=[pltpu.VMEM((tm, k), BF16)],
        compiler_params=pltpu.CompilerParams(
            dimension_semantics=("arbitrary", "arbitrary"),
            vmem_limit_bytes=VMEM_LIMIT_BYTES),
        name="norm_proj", **call_kwargs)


def _norm_proj(x, g, w, *, layer, tm, tn):
    m, k = x.shape
    n = w.shape[2]
    g = g.reshape(1, k)
    x_spec = lambda off: pl.BlockSpec((tm, k), lambda i, j: (i + off, 0))
    g_spec = pl.BlockSpec((1, k), lambda i, j: (0, 0))
    o_spec = lambda off: pl.BlockSpec((tm, tn), lambda i, j: (i + off, j))
    out = jax.ShapeDtypeStruct((m, n), BF16)
    first_only = m == tm
    res = _norm_proj_call(
        dict(emit_bf16_w=not first_only, aliased_out=False), (1, n // tn),
        [x_spec(0), g_spec, pl.BlockSpec((None, k, tn), lambda i, j: (layer, 0, j))],
        o_spec(0) if first_only else [o_spec(0), pl.BlockSpec((k, tn), lambda i, j: (0, j))],
        out if first_only else [out, jax.ShapeDtypeStruct((k, n), BF16)], tm, k,
    )(x, g, w)
    if first_only:
        return res
    out_head, w_bf16 = res
    return _norm_proj_call(
        dict(emit_bf16_w=False, aliased_out=True), (m // tm - 1, n // tn),
        [x_spec(1), g_spec, pl.BlockSpec((k, tn), lambda i, j: (0, j)),
         pl.BlockSpec(memory_space=pl.ANY)],
        o_spec(1), out, tm, k, input_output_aliases={3: 0},
    )(x, g, w_bf16, out_head)


def _attn_schedule(nq, nd):
    chains = []
    for c in range(nq // 2):
        own = (c, nq - 1 - c)
        steps = [(i, i * nd + d, s) for s, i in enumerate(own) for d in range(nd)]
        steps += [(i, j, s) for s, i in enumerate(own) for j in range(i * nd)]
        chains.append(steps)
    assert len({len(ch) for ch in chains}) == 1
    return np.asarray(chains, np.int32).transpose(1, 0, 2)


def _diff_attn_kernel(tab_ref, q_ref, k_ref, v_ref, gate_ref, lam_ref, g_ref, o_ref,
                      vt_ref, qt0_ref, qt1_ref, s0_ref, s1_ref, m_ref, acc_ref,
                      *, tq, tk, seq, lam_init):
    nq, nd = seq // tq, tq // tk
    n_chains = nq // 2
    qt_refs = (qt0_ref, qt1_ref)

    for c in range(seq // tk):
        vt_ref[:A_DV, c * tk:(c + 1) * tk] = v_ref[c * tk:(c + 1) * tk, :].T
    vt_ref[A_DV:, :] = jnp.ones((vt_ref.shape[0] - A_DV, seq), BF16)

    qscale = math.log2(math.e) * A_DQK ** -0.5
    sub = lax.broadcasted_iota(jnp.int32, (2 * A_DQK, tq), 0)
    for c in range(nq):
        qt = (q_ref[c * tq:(c + 1) * tq, :].astype(F32) * qscale).astype(BF16).T
        zero = jnp.zeros_like(qt)
        qt0_ref[:, c * tq:(c + 1) * tq] = jnp.where(sub < A_DQK, qt, zero)
        qt1_ref[:, c * tq:(c + 1) * tq] = jnp.where(sub >= A_DQK, qt, zero)

    m_ref[...] = jnp.full(m_ref.shape, -jnp.inf, F32)
    acc_ref[...] = jnp.zeros(acc_ref.shape, F32)

    s_refs = (s0_ref, s1_ref)

    assert nd == 2
    right = slice(tk, tq)

    def scores(t, par, diag=None):
        if diag is not None:
            causal = (lax.broadcasted_iota(jnp.int32, (tk, tk), 0)
                      <= lax.broadcasted_iota(jnp.int32, (tk, tk), 1))
        for c in range(n_chains):
            q0 = pl.multiple_of(tab_ref[t, c, 0] * tq, tq)
            k0 = pl.multiple_of(tab_ref[t, c, 1] * tk, tk)
            kb = k_ref[pl.ds(k0, tk), :]
            for mp in range(2):
                if diag is None:
                    s_refs[par][c, mp] = jnp.dot(kb, qt_refs[mp][:, pl.ds(q0, tq)],
                                                 preferred_element_type=F32)
                    continue
                ql = qt_refs[mp][:, pl.ds(q0, tk)]
                qr = qt_refs[mp][:, pl.ds(pl.multiple_of(q0 + tk, tk), tk)]
                if diag == 0:
                    s_refs[par][c, mp, :, :tk] = jnp.where(
                        causal, jnp.dot(kb, ql, preferred_element_type=F32), -jnp.inf)
                    s_refs[par][c, mp, :, right] = jnp.dot(kb, qr, preferred_element_type=F32)
                else:
                    s_refs[par][c, mp, :, right] = jnp.where(
                        causal, jnp.dot(kb, qr, preferred_element_type=F32), -jnp.inf)

    def update(t, par, cols=slice(None)):
        for c in range(n_chains):
            own = tab_ref[t, c, 2]
            k0 = pl.multiple_of(tab_ref[t, c, 1] * tk, tk)
            vtb = vt_ref[:, pl.ds(k0, tk)]
            for mp in range(2):
                s = s_refs[par][c, mp, :, cols]
                m_old = m_ref[c, own, mp, :, cols]
                m_new = jnp.maximum(m_old, jnp.max(s, axis=0, keepdims=True))
                p = jnp.exp2(s - m_new).astype(BF16)
                pv = jnp.dot(vtb, p, preferred_element_type=F32)
                acc_ref[c, own, mp, :, cols] = (
                    jnp.exp2(m_old - m_new) * acc_ref[c, own, mp, :, cols] + pv)
                m_ref[c, own, mp, :, cols] = m_new

    n_diag = 2 * nd
    n_trips = tab_ref.shape[0]
    assert (n_trips - n_diag) % 2 == 0
    scores(0, 0, diag=0)
    for t in range(n_diag):
        nxt = t + 1
        scores(nxt, nxt % 2, diag=(nxt % nd if nxt < n_diag else None))
        update(t, t % 2, cols=(right if t % nd == 1 else slice(None)))

    def two_trips(u, carry):
        for v in range(2):
            t = n_diag + 2 * u + v
            par = (n_diag + v) % 2
            scores(t + 1, 1 - par)
            update(t, par)
        return carry

    lax.fori_loop(0, (n_trips - n_diag - 2) // 2, two_trips, 0)
    scores(n_trips - 1, (n_trips - 1) % 2)
    update(n_trips - 2, (n_trips - 2) % 2)
    update(n_trips - 1, (n_trips - 1) % 2)

    lp = lam_ref[...]
    lam = (jnp.exp(jnp.sum(lp[0:1] * lp[1:2], axis=-1, keepdims=True))
           - jnp.exp(jnp.sum(lp[2:3] * lp[3:4], axis=-1, keepdims=True)) + lam_init)
    for c in range(n_chains):
        for own, qb in enumerate((c, nq - 1 - c)):
            inv_l1 = 1.0 / acc_ref[c, own, 0, A_DV:A_DV + 1]
            inv_l2 = lam / acc_ref[c, own, 1, A_DV:A_DV + 1]
            ot = acc_ref[c, own, 0, :A_DV] * inv_l1 - acc_ref[c, own, 1, :A_DV] * inv_l2
            ms = jnp.mean(ot * ot, axis=0, keepdims=True)
            ot = ot * lax.rsqrt(ms + EPS) * g_ref[...] * (1.0 - lam_init)
            rows = slice(qb * tq, (qb + 1) * tq)
            o_ref[rows, :] = (ot.T * jax.nn.silu(gate_ref[rows, :].astype(F32))
                              ).astype(o_ref.dtype)


def _diff_attn(p, lam_p, subln_g, *, batch, seq, lam_init, tq, tk):
    t = batch * seq
    nq, nd = seq // tq, tq // tk
    assert nq % 2 == 0
    table = _attn_schedule(nq, nd)
    kernel = functools.partial(_diff_attn_kernel, tq=tq, tk=tk, seq=seq, lam_init=lam_init)
    acc_rows = A_DV + BF16_SUBLANES
    return pl.pallas_call(
        kernel,
        grid=(batch, A_HEADS),
        in_specs=[
            pl.BlockSpec(memory_space=pltpu.SMEM),
            pl.BlockSpec((seq, 128), lambda b, h: (b, COL_QA + h)),
            pl.BlockSpec((seq, 128), lambda b, h: (b, COL_KA + h)),
            pl.BlockSpec((seq, 128), lambda b, h: (b, COL_VA + h)),
            pl.BlockSpec((seq, 128), lambda b, h: (b, COL_GA + h)),
            pl.BlockSpec((4, A_DQK), lambda b, h: (0, 0)),
            pl.BlockSpec((A_DV, 1), lambda b, h: (0, 0)),
        ],
        out_specs=pl.BlockSpec((seq, 128), lambda b, h: (b, h)),
        out_shape=jax.ShapeDtypeStruct((t, BRANCH_WIDTH), BF16),
        scratch_shapes=[
            pltpu.VMEM((acc_rows, seq), BF16),
            pltpu.VMEM((2 * A_DQK, seq), BF16),
            pltpu.VMEM((2 * A_DQK, seq), BF16),
            pltpu.VMEM((nq // 2, 2, tk, tq), F32),
            pltpu.VMEM((nq // 2, 2, tk, tq), F32),
            pltpu.VMEM((nq // 2, 2, 2, 1, tq), F32),
            pltpu.VMEM((nq // 2, 2, 2, acc_rows, tq), F32),
        ],
        compiler_params=pltpu.CompilerParams(
            dimension_semantics=("arbitrary", "arbitrary"),
            vmem_limit_bytes=VMEM_LIMIT_BYTES),
        name="diff_attn",
    )(jnp.asarray(table), p, p, p, p, lam_p, subln_g.reshape(A_DV, 1))


def _block_cumsum(x, rows, blk):
    r = lax.broadcasted_iota(jnp.int32, (rows, rows), 0)
    c = lax.broadcasted_iota(jnp.int32, (rows, rows), 1)
    tri = jnp.where((r // blk == c // blk) & (c <= r), 1.0, 0.0).astype(BF16)
    hi = x.astype(BF16)
    rem = x - hi.astype(F32)
    mid = rem.astype(BF16)
    lo = (rem - mid.astype(F32)).astype(BF16)
    return (jnp.dot(tri, hi, preferred_element_type=F32)
            + jnp.dot(tri, mid, preferred_element_type=F32)
            + jnp.dot(tri, lo, preferred_element_type=F32))


def _hgrn_kernel(f_ref, i_ref, q_ref, gate_ref, lbraw_ref, ng_ref, o_ref,
                 st_ref, b_s, q_s, lf_s, k_s, o_s, *, layer, rows):
    @pl.when(pl.program_id(1) == 0)
    def _():
        st_ref[...] = jnp.zeros(st_ref.shape, F32)

    raw = lbraw_ref[...]
    e = jnp.exp(raw - jnp.max(raw, axis=0, keepdims=True))
    w = e / jnp.sum(e, axis=0, keepdims=True)
    lb = jnp.zeros((1, raw.shape[1]), F32)
    for j in range(1, layer + 1):
        lb = lb + w[j:j + 1]

    z = f_ref[...].astype(F32)
    e = jnp.exp(-jnp.abs(z))
    inv = 1.0 / (1.0 + e)
    log_sig = jnp.minimum(z, 0.0) + jnp.log(inv)
    k_s[...] = (1.0 - lb) * (jnp.where(z >= 0.0, e, 1.0) * inv)
    a = jnp.log(lb)
    c = jnp.log1p(-lb) + log_sig
    logf = jnp.maximum(a, c) + jnp.log(1.0 + jnp.exp(-jnp.abs(a - c)))
    qf = q_ref[...].astype(F32)
    q_s[...] = qf / (1.0 + jnp.exp(-qf))
    lf_s[...] = logf

    b_s[...] = _block_cumsum(logf, rows, CHUNK)
    totals = jnp.concatenate(
        [b_s[c * CHUNK + CHUNK - 1:(c + 1) * CHUNK, :] for c in range(rows // CHUNK)], axis=0)
    fast = jnp.max(-totals) <= MAX_CHUNK_DECAY

    @pl.when(fast)
    def _():
        t_i = lax.broadcasted_iota(jnp.int32, (CHUNK, CHUNK), 0)
        s_i = lax.broadcasted_iota(jnp.int32, (CHUNK, CHUNK), 1)
        causal = s_i <= t_i
        for c in range(rows // CHUNK):
            rs = slice(c * CHUNK, (c + 1) * CHUNK)
            weights, inter = [], []
            for h in range(B_HEADS):
                cs = slice(h * B_DK, (h + 1) * B_DK)
                bn, qn, kn, vn = b_s[rs, cs], q_s[rs, cs], k_s[rs, cs], i_ref[rs, cs]
                bl = bn[CHUNK - 1:CHUNK]
                half = 0.5 * bl
                qp = (qn * jnp.exp(bn - half)).astype(BF16)
                kp = (kn * jnp.exp(half - bn)).astype(BF16)
                weights.append(_nt_dot(qp, kp))
                st = st_ref[h]
                qe = (qn * jnp.exp(bn)).astype(BF16)
                inter.append(_nt_dot(qe, st.astype(BF16)))
                kd = (kn * jnp.exp(bl - bn)).astype(BF16)
                st_ref[h] = st * jnp.exp(bl) + _tn_dot(vn, kd)
            for h in range(B_HEADS):
                cs = slice(h * B_DK, (h + 1) * B_DK)
                a = jnp.where(causal, weights[h], 0.0).astype(BF16)
                o_s[rs, cs] = inter[h] + jnp.dot(a, i_ref[rs, cs], preferred_element_type=F32)

    @pl.when(jnp.logical_not(fast))
    def _():
        b_s[...] = _block_cumsum(lf_s[...], rows, SUB)
        t_idx = lax.broadcasted_iota(jnp.int32, (SUB, 1), 0)

        def body(n, carry):
            r0 = pl.multiple_of(n * SUB, SUB)
            for h in range(B_HEADS):
                cs = slice(h * B_DK, (h + 1) * B_DK)
                bn = b_s[pl.ds(r0, SUB), cs]
                qn = q_s[pl.ds(r0, SUB), cs]
                kn = k_s[pl.ds(r0, SUB), cs]
                vn = i_ref[pl.ds(r0, SUB), cs]
                vf = vn.astype(F32)
                st = st_ref[h]
                acc = _nt_dot((qn * jnp.exp(bn)).astype(BF16), st.astype(BF16))
                for s in range(SUB):
                    arg = jnp.where(t_idx >= s, bn - bn[s:s + 1], -jnp.inf)
                    wts = qn * (kn[s:s + 1] * jnp.exp(arg))
                    acc = acc + jnp.sum(wts, axis=-1, keepdims=True) * vf[s:s + 1]
                o_s[pl.ds(r0, SUB), cs] = acc
                bl = bn[SUB - 1:SUB]
                kd = kn * jnp.exp(bl - bn)
                st_ref[h] = st * jnp.exp(bl) + _tn_dot(vn, kd.astype(BF16))
            return carry

        lax.fori_loop(0, rows // SUB, body, 0)

    for h in range(B_HEADS):
        cs = slice(h * B_DK, (h + 1) * B_DK)
        o = o_s[:, cs]
        ms = jnp.mean(o * o, axis=-1, keepdims=True)
        y = o * lax.rsqrt(ms + EPS) * ng_ref[...]
        o_ref[:, cs] = (y * jax.nn.silu(gate_ref[:, cs].astype(F32))).astype(o_ref.dtype)


def _hgrn(p, lb_raw, norm_g, *, batch, seq, layer, rows):
    t = batch * seq
    nr = seq // rows
    kernel = functools.partial(_hgrn_kernel, layer=layer, rows=rows)
    blk = lambda col: pl.BlockSpec((rows, 1024), lambda b, i: (b * nr + i, col // 8))
    depth = lb_raw.shape[0]
    return pl.pallas_call(
        kernel,
        grid=(batch, nr),
        in_specs=[
            blk(COL_FB), blk(COL_IB), blk(COL_QB), blk(COL_GB),
            pl.BlockSpec((depth, 1024), lambda b, i: (0, 0)),
            pl.BlockSpec((1, B_DK), lambda b, i: (0, 0)),
        ],
        out_specs=pl.BlockSpec((rows, 1024), lambda b, i: (b * nr + i, 0)),
        out_shape=jax.ShapeDtypeStruct((t, BRANCH_WIDTH), BF16),
        scratch_shapes=[
            pltpu.VMEM((B_HEADS, B_DK, B_DK), F32),
            pltpu.VMEM((rows, 1024), F32),
            pltpu.VMEM((rows, 1024), F32),
            pltpu.VMEM((rows, 1024), F32),
            pltpu.VMEM((rows, 1024), F32),
            pltpu.VMEM((rows, 1024), F32),
        ],
        compiler_params=pltpu.CompilerParams(
            dimension_semantics=("arbitrary", "arbitrary"),
            vmem_limit_bytes=VMEM_LIMIT_BYTES),
        name="hgrn2",
    )(p, p, p, p, lb_raw, norm_g.reshape(1, B_DK))


def _xattn_kernel(q_ref, gate_ref, k_ref, v_ref, o_ref):
    heads = [slice(h * C_DH, (h + 1) * C_DH) for h in range(C_HEADS)]
    scores = [_nt_dot(q_ref[:, cs], k_ref[:, cs]) for cs in heads]
    c = (C_DH ** -0.5) * math.log2(math.e)
    for cs, s in zip(heads, scores):
        m = jnp.max(s, axis=-1, keepdims=True)
        pr = jnp.exp2((s - m) * c)
        inv_l = 1.0 / jnp.sum(pr, axis=-1, keepdims=True)
        o = jnp.dot(pr.astype(BF16), v_ref[:, cs], preferred_element_type=F32) * inv_l
        o_ref[:, cs] = (o * jax.nn.silu(gate_ref[:, cs].astype(F32))).astype(o_ref.dtype)


def _xattn(p, kv, *, batch, seq, tq):
    t = batch * seq
    nq = seq // tq
    return pl.pallas_call(
        _xattn_kernel,
        grid=(batch, nq),
        in_specs=[
            pl.BlockSpec((tq, 1024), lambda b, i: (b * nq + i, COL_QC_1024)),
            pl.BlockSpec((tq, 1024), lambda b, i: (b * nq + i, COL_GC_1024)),
            pl.BlockSpec((N_MEM, 1024), lambda b, i: (b, 0)),
            pl.BlockSpec((N_MEM, 1024), lambda b, i: (b, 1)),
        ],
        out_specs=pl.BlockSpec((tq, 1024), lambda b, i: (b * nq + i, 0)),
        out_shape=jax.ShapeDtypeStruct((t, BRANCH_WIDTH), BF16),
        compiler_params=pltpu.CompilerParams(
            dimension_semantics=("arbitrary", "arbitrary"),
            vmem_limit_bytes=VMEM_LIMIT_BYTES),
        name="mem_xattn",
    )(p, p, kv, kv)


def _merge_kernel(oa_ref, ob_ref, oc_ref, g0_ref, g1_ref, g2_ref, x_ref, wb_ref, wo_ref,
                  fg_ref, o_ref, *, final):
    half = D_MODEL // 2
    ys = []
    for cols in (slice(0, half), slice(half, D_MODEL)):
        y = None
        for o_r, g_r, j in ((oa_ref, g0_ref, 0), (ob_ref, g1_ref, 1), (oc_ref, g2_ref, 2)):
            term = jax.nn.sigmoid(g_r[:, cols].astype(F32)) * jnp.dot(
                o_r[...], wb_ref[j, :, cols], preferred_element_type=F32)
            y = term if y is None else y + term
        ys.append(y.astype(BF16))
    xn = x_ref[...] + jnp.dot(ys[0], wo_ref[:half, :], preferred_element_type=F32)
    xn = xn + jnp.dot(ys[1], wo_ref[half:, :], preferred_element_type=F32)
    if final:
        ms = jnp.mean(xn * xn, axis=-1, keepdims=True)
        xn = xn * lax.rsqrt(ms + EPS) * fg_ref[...]
    o_ref[...] = xn


def _merge(oa, ob, oc, p, x, wb, wo, final_g, *, layer, tm, final):
    t = x.shape[0]
    row = lambda i: (i, 0)
    const2 = lambda i: (0, 0)
    kernel = functools.partial(_merge_kernel, final=final)
    return pl.pallas_call(
        kernel,
        grid=(t // tm,),
        in_specs=[
            pl.BlockSpec((tm, BRANCH_WIDTH), row),
            pl.BlockSpec((tm, BRANCH_WIDTH), row),
            pl.BlockSpec((tm, BRANCH_WIDTH), row),
            pl.BlockSpec((tm, D_MODEL), lambda i: (i, COL_GL_2048)),
            pl.BlockSpec((tm, D_MODEL), lambda i: (i, COL_GL_2048 + 1)),
            pl.BlockSpec((tm, D_MODEL), lambda i: (i, COL_GL_2048 + 2)),
            pl.BlockSpec((tm, D_MODEL), row),
            pl.BlockSpec((None, N_BRANCH, BRANCH_WIDTH, D_MODEL), lambda i: (layer, 0, 0, 0),
                         pipeline_mode=pl.Buffered(1)),
            pl.BlockSpec((None, D_MODEL, D_MODEL), lambda i: (layer, 0, 0),
                         pipeline_mode=pl.Buffered(1)),
            pl.BlockSpec((1, D_MODEL), const2),
        ],
        out_specs=pl.BlockSpec((tm, D_MODEL), row),
        out_shape=jax.ShapeDtypeStruct((t, D_MODEL), F32),
        compiler_params=pltpu.CompilerParams(
            dimension_semantics=("arbitrary",),
            vmem_limit_bytes=VMEM_LIMIT_BYTES),
        name="merge_out",
    )(oa, ob, oc, p, p, p, x, wb, wo, final_g.reshape(1, D_MODEL))


def kernel(x, mem, norm_g, w_in, diff_lambda, diff_subln_g, hgrn_lb_raw, hgrn_norm_g,
           mem_norm_g, w_kv_mem, w_branch, w_out, final_norm_g):
    batch, seq, d = x.shape
    depth = w_in.shape[0]
    xt = x.reshape(batch * seq, d)
    memt = mem.reshape(batch * N_MEM, d)
    w_br_b = w_branch.astype(BF16)
    w_out_b = w_out.astype(BF16)
    lb_raw = hgrn_lb_raw.astype(F32)
    for l in range(depth):
        lam_init = 0.8 - 0.6 * math.exp(-0.3 * l)
        p = _norm_proj(xt, norm_g[l], w_in, layer=l, tm=1024, tn=1024)
        kv = _norm_proj(memt, mem_norm_g[l], w_kv_mem, layer=l, tm=1024, tn=1024)
        oa = _diff_attn(p, diff_lambda[l], diff_subln_g[l], batch=batch, seq=seq,
                        lam_init=lam_init, tq=512, tk=256)
        ob = _hgrn(p, lb_raw, hgrn_norm_g[l], batch=batch, seq=seq, layer=l, rows=256)
        oc = _xattn(p, kv, batch=batch, seq=seq, tq=512)
        xt = _merge(oa, ob, oc, p, xt, w_br_b, w_out_b, final_norm_g,
                    layer=l, tm=256, final=(l == depth - 1))
    return xt.reshape(batch, seq, d)
```

```python
import functools
import math

import jax
import jax.numpy as jnp
import numpy as np
from jax import lax
from jax.experimental import pallas as pl
from jax.experimental.pallas import tpu as pltpu

F32 = jnp.float32
BF16 = jnp.bfloat16

D_MODEL = 2048
N_MEM = 256
A_HEADS = 8
A_DQK = 64
A_DV = 128
B_HEADS = 8
B_DK = 128
C_HEADS = 4
C_DH = 256
BRANCH_WIDTH = 1024
N_BRANCH = 3
EPS = 1e-6

COL_QA, COL_KA, COL_VA, COL_GA = 0, 8, 16, 24
COL_FB, COL_IB, COL_QB, COL_GB = 32, 40, 48, 56
COL_QC_1024, COL_GC_1024 = 8, 9
COL_GL_2048 = 5

VMEM_LIMIT_BYTES = 56 * 1024 * 1024

BF16_SUBLANES = 16
CHUNK = 64
MAX_CHUNK_DECAY = 120.0
SUB = 16


LOG2E = math.log2(math.e)


def _sigmoid(x):
    return 0.5 * jnp.tanh(0.5 * x) + 0.5


def _silu(x):
    return x * _sigmoid(x)


def _nt_dot(a, b):
    return lax.dot_general(a, b, (((1,), (1,)), ((), ())), preferred_element_type=F32)


def _tn_dot(a, b):
    return lax.dot_general(a, b, (((0,), (0,)), ((), ())), preferred_element_type=F32)


def _norm_proj_kernel(x_ref, g_ref, w_ref, *refs, emit_bf16_w, aliased_out):
    refs = refs[1:] if aliased_out else refs
    o_ref, h_ref = refs[0], refs[-1]

    @pl.when(pl.program_id(1) == 0)
    def _():
        x = x_ref[...]
        ms = jnp.mean(x * x, axis=-1, keepdims=True)
        h_ref[...] = (x * lax.rsqrt(ms + EPS) * g_ref[...]).astype(BF16)

    w = w_ref[...].astype(BF16)
    if emit_bf16_w:
        refs[1][...] = w
    o_ref[...] = jnp.dot(h_ref[...], w, preferred_element_type=F32).astype(o_ref.dtype)


def _norm_proj_call(kernel_kwargs, grid, in_specs, out_specs, out_shape, tm, k, **call_kwargs):
    return pl.pallas_call(
        functools.partial(_norm_proj_kernel, **kernel_kwargs),
        grid=grid, in_specs=in_specs, out_specs=out_specs, out_shape=out_shape,
        scratch_shapes=[pltpu.VMEM((tm, k), BF16)],
        compiler_params=pltpu.CompilerParams(
            dimension_semantics=("arbitrary", "arbitrary"),
            vmem_limit_bytes=VMEM_LIMIT_BYTES),
        name="norm_proj", **call_kwargs)


def _norm_proj(x, g, w, *, layer, tm, tn, tn_rest=None):
    m, k = x.shape
    n = w.shape[2]
    g = g.reshape(1, k)
    x_spec = lambda off: pl.BlockSpec((tm, k), lambda i, j: (i + off, 0))
    g_spec = pl.BlockSpec((1, k), lambda i, j: (0, 0))
    o_spec = lambda off, tn=tn: pl.BlockSpec((tm, tn), lambda i, j: (i + off, j))
    out = jax.ShapeDtypeStruct((m, n), BF16)
    first_only = m == tm
    res = _norm_proj_call(
        dict(emit_bf16_w=not first_only, aliased_out=False), (1, n // tn),
        [x_spec(0), g_spec, pl.BlockSpec((None, k, tn), lambda i, j: (layer, 0, j))],
        o_spec(0) if first_only else [o_spec(0), pl.BlockSpec((k, tn), lambda i, j: (0, j))],
        out if first_only else [out, jax.ShapeDtypeStruct((k, n), BF16)], tm, k,
    )(x, g, w)
    if first_only:
        return res
    out_head, w_bf16 = res
    return _norm_proj_call(
        dict(emit_bf16_w=False, aliased_out=True), (m // tm - 1, n // tn_rest),
        [x_spec(1), g_spec, pl.BlockSpec((k, tn_rest), lambda i, j: (0, j)),
         pl.BlockSpec(memory_space=pl.ANY)],
        o_spec(1, tn_rest), out, tm, k, input_output_aliases={3: 0},
    )(x, g, w_bf16, out_head)


def _attn_schedule(nq, nd):
    chains = []
    for c in range(nq // 2):
        own = (c, nq - 1 - c)
        steps = [(i, i * nd + d, s) for s, i in enumerate(own) for d in range(nd)]
        steps += [(i, j, s) for s, i in enumerate(own) for j in range(i * nd)]
        chains.append(steps)
    assert len({len(ch) for ch in chains}) == 1
    return np.asarray(chains, np.int32).transpose(1, 0, 2)


def _diff_attn_kernel(tab_ref, q_ref, k_ref, v_ref, gate_ref, lam_ref, g_ref, o_ref,
                      vt_ref, qt0_ref, qt1_ref, s0_ref, s1_ref, m_ref, acc_ref,
                      *, tq, tk, seq, lam_init):
    nq, nd = seq // tq, tq // tk
    n_chains = nq // 2
    qt_refs = (qt0_ref, qt1_ref)

    for c in range(seq // tk):
        vt_ref[:A_DV, c * tk:(c + 1) * tk] = v_ref[c * tk:(c + 1) * tk, :].T
    vt_ref[A_DV:, :] = jnp.ones((vt_ref.shape[0] - A_DV, seq), BF16)

    qscale = math.log2(math.e) * A_DQK ** -0.5
    sub = lax.broadcasted_iota(jnp.int32, (2 * A_DQK, tq), 0)
    for c in range(nq):
        qt = (q_ref[c * tq:(c + 1) * tq, :].astype(F32) * qscale).astype(BF16).T
        zero = jnp.zeros_like(qt)
        qt0_ref[:, c * tq:(c + 1) * tq] = jnp.where(sub < A_DQK, qt, zero)
        qt1_ref[:, c * tq:(c + 1) * tq] = jnp.where(sub >= A_DQK, qt, zero)

    m_ref[...] = jnp.full(m_ref.shape, -jnp.inf, F32)
    acc_ref[...] = jnp.zeros(acc_ref.shape, F32)

    s_refs = (s0_ref, s1_ref)

    assert nd == 2
    right = slice(tk, tq)

    def scores(t, par, diag=None):
        if diag is not None:
            causal = (lax.broadcasted_iota(jnp.int32, (tk, tk), 0)
                      <= lax.broadcasted_iota(jnp.int32, (tk, tk), 1))
        for c in range(n_chains):
            q0 = pl.multiple_of(tab_ref[t, c, 0] * tq, tq)
            k0 = pl.multiple_of(tab_ref[t, c, 1] * tk, tk)
            kb = k_ref[pl.ds(k0, tk), :]
            for mp in range(2):
                if diag is None:
                    s_refs[par][c, mp] = jnp.dot(kb, qt_refs[mp][:, pl.ds(q0, tq)],
                                                 preferred_element_type=F32)
                    continue
                ql = qt_refs[mp][:, pl.ds(q0, tk)]
                qr = qt_refs[mp][:, pl.ds(pl.multiple_of(q0 + tk, tk), tk)]
                if diag == 0:
                    s_refs[par][c, mp, :, :tk] = jnp.where(
                        causal, jnp.dot(kb, ql, preferred_element_type=F32), -jnp.inf)
                    s_refs[par][c, mp, :, right] = jnp.dot(kb, qr, preferred_element_type=F32)
                else:
                    s_refs[par][c, mp, :, right] = jnp.where(
                        causal, jnp.dot(kb, qr, preferred_element_type=F32), -jnp.inf)

    def update(t, par, cols=slice(None)):
        for c in range(n_chains):
            own = tab_ref[t, c, 2]
            k0 = pl.multiple_of(tab_ref[t, c, 1] * tk, tk)
            vtb = vt_ref[:, pl.ds(k0, tk)]
            for mp in range(2):
                s = s_refs[par][c, mp, :, cols]
                m_old = m_ref[c, own, mp, :, cols]
                m_new = jnp.maximum(m_old, jnp.max(s, axis=0, keepdims=True))
                p = jnp.exp2(s - m_new).astype(BF16)
                pv = jnp.dot(vtb, p, preferred_element_type=F32)
                acc_ref[c, own, mp, :, cols] = (
                    jnp.exp2(m_old - m_new) * acc_ref[c, own, mp, :, cols] + pv)
                m_ref[c, own, mp, :, cols] = m_new

    n_diag = 2 * nd
    n_trips = tab_ref.shape[0]
    assert (n_trips - n_diag) % 2 == 0
    scores(0, 0, diag=0)
    for t in range(n_diag):
        nxt = t + 1
        scores(nxt, nxt % 2, diag=(nxt % nd if nxt < n_diag else None))
        update(t, t % 2, cols=(right if t % nd == 1 else slice(None)))

    def two_trips(u, carry):
        for v in range(2):
            t = n_diag + 2 * u + v
            par = (n_diag + v) % 2
            scores(t + 1, 1 - par)
            update(t, par)
        return carry

    lax.fori_loop(0, (n_trips - n_diag - 2) // 2, two_trips, 0)
    scores(n_trips - 1, (n_trips - 1) % 2)
    update(n_trips - 2, (n_trips - 2) % 2)
    update(n_trips - 1, (n_trips - 1) % 2)

    lp = lam_ref[...]
    lam = (jnp.exp(jnp.sum(lp[0:1] * lp[1:2], axis=-1, keepdims=True))
           - jnp.exp(jnp.sum(lp[2:3] * lp[3:4], axis=-1, keepdims=True)) + lam_init)
    for c in range(n_chains):
        for own, qb in enumerate((c, nq - 1 - c)):
            inv_l1 = 1.0 / acc_ref[c, own, 0, A_DV:A_DV + 1]
            inv_l2 = lam / acc_ref[c, own, 1, A_DV:A_DV + 1]
            ot = acc_ref[c, own, 0, :A_DV] * inv_l1 - acc_ref[c, own, 1, :A_DV] * inv_l2
            ms = jnp.mean(ot * ot, axis=0, keepdims=True)
            ot = ot * (lax.rsqrt(ms + EPS) * (1.0 - lam_init)) * g_ref[...]
            rows = slice(qb * tq, (qb + 1) * tq)
            o_ref[rows, :] = (ot.T * _silu(gate_ref[rows, :].astype(F32))
                              ).astype(o_ref.dtype)


def _diff_attn(p, lam_p, subln_g, *, batch, seq, lam_init, tq, tk):
    t = batch * seq
    nq, nd = seq // tq, tq // tk
    assert nq % 2 == 0
    table = _attn_schedule(nq, nd)
    kernel = functools.partial(_diff_attn_kernel, tq=tq, tk=tk, seq=seq, lam_init=lam_init)
    acc_rows = A_DV + BF16_SUBLANES
    return pl.pallas_call(
        kernel,
        grid=(batch, A_HEADS),
        in_specs=[
            pl.BlockSpec(memory_space=pltpu.SMEM),
            pl.BlockSpec((seq, 128), lambda b, h: (b, COL_QA + h)),
            pl.BlockSpec((seq, 128), lambda b, h: (b, COL_KA + h)),
            pl.BlockSpec((seq, 128), lambda b, h: (b, COL_VA + h)),
            pl.BlockSpec((seq, 128), lambda b, h: (b, COL_GA + h)),
            pl.BlockSpec((4, A_DQK), lambda b, h: (0, 0)),
            pl.BlockSpec((A_DV, 1), lambda b, h: (0, 0)),
        ],
        out_specs=pl.BlockSpec((seq, 128), lambda b, h: (b, h)),
        out_shape=jax.ShapeDtypeStruct((t, BRANCH_WIDTH), BF16),
        scratch_shapes=[
            pltpu.VMEM((acc_rows, seq), BF16),
            pltpu.VMEM((2 * A_DQK, seq), BF16),
            pltpu.VMEM((2 * A_DQK, seq), BF16),
            pltpu.VMEM((nq // 2, 2, tk, tq), F32),
            pltpu.VMEM((nq // 2, 2, tk, tq), F32),
            pltpu.VMEM((nq // 2, 2, 2, 1, tq), F32),
            pltpu.VMEM((nq // 2, 2, 2, acc_rows, tq), F32),
        ],
        compiler_params=pltpu.CompilerParams(
            dimension_semantics=("arbitrary", "arbitrary"),
            vmem_limit_bytes=VMEM_LIMIT_BYTES),
        name="diff_attn",
    )(jnp.asarray(table), p, p, p, p, lam_p, subln_g.reshape(A_DV, 1))


def _block_cumsum(x, rows, blk):
    r = lax.broadcasted_iota(jnp.int32, (rows, rows), 0)
    c = lax.broadcasted_iota(jnp.int32, (rows, rows), 1)
    tri = jnp.where((r // blk == c // blk) & (c <= r), 1.0, 0.0).astype(BF16)
    hi = x.astype(BF16)
    rem = x - hi.astype(F32)
    mid = rem.astype(BF16)
    lo = (rem - mid.astype(F32)).astype(BF16)
    return (jnp.dot(tri, hi, preferred_element_type=F32)
            + jnp.dot(tri, mid, preferred_element_type=F32)
            + jnp.dot(tri, lo, preferred_element_type=F32))


def _hgrn_kernel(f_ref, i_ref, q_ref, gate_ref, lbraw_ref, ng_ref, o_ref,
                 st_ref, b_s, q_s, lf_s, k_s, o_s, *, layer, rows):
    @pl.when(pl.program_id(1) == 0)
    def _():
        st_ref[...] = jnp.zeros(st_ref.shape, F32)

    raw = lbraw_ref[...]
    e = jnp.exp(raw - jnp.max(raw, axis=0, keepdims=True))
    w = e / jnp.sum(e, axis=0, keepdims=True)
    lb = jnp.zeros((1, raw.shape[1]), F32)
    for j in range(1, layer + 1):
        lb = lb + w[j:j + 1]

    z = f_ref[...].astype(F32)
    e = jnp.exp2(jnp.abs(z) * -LOG2E)
    inv = 1.0 / (1.0 + e)
    log_sig = jnp.minimum(z, 0.0) * LOG2E + jnp.log2(inv)
    k_s[...] = (1.0 - lb) * (jnp.where(z >= 0.0, e, 1.0) * inv)
    a = jnp.log(lb) * LOG2E
    c = jnp.log1p(-lb) * LOG2E + log_sig
    logf = jnp.maximum(a, c) + jnp.log2(1.0 + jnp.exp2(-jnp.abs(a - c)))
    q_s[...] = _silu(q_ref[...].astype(F32))
    lf_s[...] = logf

    b_s[...] = _block_cumsum(logf, rows, CHUNK)
    totals = jnp.concatenate(
        [b_s[c * CHUNK + CHUNK - 1:(c + 1) * CHUNK, :] for c in range(rows // CHUNK)], axis=0)
    fast = jnp.max(-totals) <= MAX_CHUNK_DECAY * LOG2E

    @pl.when(fast)
    def _():
        t_i = lax.broadcasted_iota(jnp.int32, (CHUNK, CHUNK), 0)
        s_i = lax.broadcasted_iota(jnp.int32, (CHUNK, CHUNK), 1)
        causal = s_i <= t_i
        for c in range(rows // CHUNK):
            rs = slice(c * CHUNK, (c + 1) * CHUNK)
            weights, inter = [], []
            for h in range(B_HEADS):
                cs = slice(h * B_DK, (h + 1) * B_DK)
                bn, qn, kn, vn = b_s[rs, cs], q_s[rs, cs], k_s[rs, cs], i_ref[rs, cs]
                bl = bn[CHUNK - 1:CHUNK]
                half = 0.5 * bl
                qp = (qn * jnp.exp2(bn - half)).astype(BF16)
                kp = (kn * jnp.exp2(half - bn)).astype(BF16)
                weights.append(_nt_dot(qp, kp))
                st = st_ref[h]
                qe = (qn * jnp.exp2(bn)).astype(BF16)
                inter.append(_nt_dot(qe, st.astype(BF16)))
                kd = (kn * jnp.exp2(bl - bn)).astype(BF16)
                st_ref[h] = st * jnp.exp2(bl) + _tn_dot(vn, kd)
            for h in range(B_HEADS):
                cs = slice(h * B_DK, (h + 1) * B_DK)
                a = jnp.where(causal, weights[h], 0.0).astype(BF16)
                o_s[rs, cs] = inter[h] + jnp.dot(a, i_ref[rs, cs], preferred_element_type=F32)

    @pl.when(jnp.logical_not(fast))
    def _():
        b_s[...] = _block_cumsum(lf_s[...], rows, SUB)
        t_idx = lax.broadcasted_iota(jnp.int32, (SUB, 1), 0)

        def body(n, carry):
            r0 = pl.multiple_of(n * SUB, SUB)
            for h in range(B_HEADS):
                cs = slice(h * B_DK, (h + 1) * B_DK)
                bn = b_s[pl.ds(r0, SUB), cs]
                qn = q_s[pl.ds(r0, SUB), cs]
                kn = k_s[pl.ds(r0, SUB), cs]
                vn = i_ref[pl.ds(r0, SUB), cs]
                vf = vn.astype(F32)
                st = st_ref[h]
                acc = _nt_dot((qn * jnp.exp2(bn)).astype(BF16), st.astype(BF16))
                for s in range(SUB):
                    arg = jnp.where(t_idx >= s, bn - bn[s:s + 1], -jnp.inf)
                    wts = qn * (kn[s:s + 1] * jnp.exp2(arg))
                    acc = acc + jnp.sum(wts, axis=-1, keepdims=True) * vf[s:s + 1]
                o_s[pl.ds(r0, SUB), cs] = acc
                bl = bn[SUB - 1:SUB]
                kd = kn * jnp.exp2(bl - bn)
                st_ref[h] = st * jnp.exp2(bl) + _tn_dot(vn, kd.astype(BF16))
            return carry

        lax.fori_loop(0, rows // SUB, body, 0)

    for h in range(B_HEADS):
        cs = slice(h * B_DK, (h + 1) * B_DK)
        o = o_s[:, cs]
        ms = jnp.mean(o * o, axis=-1, keepdims=True)
        y = o * lax.rsqrt(ms + EPS) * ng_ref[...]
        o_ref[:, cs] = (y * _silu(gate_ref[:, cs].astype(F32))).astype(o_ref.dtype)


def _hgrn(p, lb_raw, norm_g, *, batch, seq, layer, rows):
    t = batch * seq
    nr = seq // rows
    kernel = functools.partial(_hgrn_kernel, layer=layer, rows=rows)
    blk = lambda col: pl.BlockSpec((rows, 1024), lambda b, i: (b * nr + i, col // 8))
    depth = lb_raw.shape[0]
    return pl.pallas_call(
        kernel,
        grid=(batch, nr),
        in_specs=[
            blk(COL_FB), blk(COL_IB), blk(COL_QB), blk(COL_GB),
            pl.BlockSpec((depth, 1024), lambda b, i: (0, 0)),
            pl.BlockSpec((1, B_DK), lambda b, i: (0, 0)),
        ],
        out_specs=pl.BlockSpec((rows, 1024), lambda b, i: (b * nr + i, 0)),
        out_shape=jax.ShapeDtypeStruct((t, BRANCH_WIDTH), BF16),
        scratch_shapes=[
            pltpu.VMEM((B_HEADS, B_DK, B_DK), F32),
            pltpu.VMEM((rows, 1024), F32),
            pltpu.VMEM((rows, 1024), F32),
            pltpu.VMEM((rows, 1024), F32),
            pltpu.VMEM((rows, 1024), F32),
            pltpu.VMEM((rows, 1024), F32),
        ],
        compiler_params=pltpu.CompilerParams(
            dimension_semantics=("arbitrary", "arbitrary"),
            vmem_limit_bytes=VMEM_LIMIT_BYTES),
        name="hgrn2",
    )(p, p, p, p, lb_raw, norm_g.reshape(1, B_DK))


def _xattn_kernel(q_ref, gate_ref, k_ref, v_ref, o_ref):
    heads = [slice(h * C_DH, (h + 1) * C_DH) for h in range(C_HEADS)]
    scores = [_nt_dot(q_ref[:, cs], k_ref[:, cs]) for cs in heads]
    c = (C_DH ** -0.5) * math.log2(math.e)
    for cs, s in zip(heads, scores):
        m = jnp.max(s, axis=-1, keepdims=True)
        pr = jnp.exp2((s - m) * c)
        inv_l = 1.0 / jnp.sum(pr, axis=-1, keepdims=True)
        o = jnp.dot(pr.astype(BF16), v_ref[:, cs], preferred_element_type=F32) * inv_l
        o_ref[:, cs] = (o * _silu(gate_ref[:, cs].astype(F32))).astype(o_ref.dtype)


def _xattn(p, kv, *, batch, seq, tq):
    t = batch * seq
    nq = seq // tq
    return pl.pallas_call(
        _xattn_kernel,
        grid=(batch, nq),
        in_specs=[
            pl.BlockSpec((tq, 1024), lambda b, i: (b * nq + i, COL_QC_1024)),
            pl.BlockSpec((tq, 1024), lambda b, i: (b * nq + i, COL_GC_1024)),
            pl.BlockSpec((N_MEM, 1024), lambda b, i: (b, 0)),
            pl.BlockSpec((N_MEM, 1024), lambda b, i: (b, 1)),
        ],
        out_specs=pl.BlockSpec((tq, 1024), lambda b, i: (b * nq + i, 0)),
        out_shape=jax.ShapeDtypeStruct((t, BRANCH_WIDTH), BF16),
        compiler_params=pltpu.CompilerParams(
            dimension_semantics=("arbitrary", "arbitrary"),
            vmem_limit_bytes=VMEM_LIMIT_BYTES),
        name="mem_xattn",
    )(p, p, kv, kv)


def _merge_kernel(oa_ref, ob_ref, oc_ref, g0_ref, g1_ref, g2_ref, x_ref, wb_ref, wo_ref,
                  fg_ref, o_ref, *, final):
    half = D_MODEL // 2
    ys = []
    for cols in (slice(0, half), slice(half, D_MODEL)):
        y = None
        for o_r, g_r, j in ((oa_ref, g0_ref, 0), (ob_ref, g1_ref, 1), (oc_ref, g2_ref, 2)):
            term = _sigmoid(g_r[:, cols].astype(F32)) * jnp.dot(
                o_r[...], wb_ref[j, :, cols], preferred_element_type=F32)
            y = term if y is None else y + term
        ys.append(y.astype(BF16))
    xn = x_ref[...] + jnp.dot(ys[0], wo_ref[:half, :], preferred_element_type=F32)
    xn = xn + jnp.dot(ys[1], wo_ref[half:, :], preferred_element_type=F32)
    if final:
        ms = jnp.mean(xn * xn, axis=-1, keepdims=True)
        xn = xn * lax.rsqrt(ms + EPS) * fg_ref[...]
    o_ref[...] = xn


def _merge(oa, ob, oc, p, x, wb, wo, final_g, *, layer, tm, final):
    t = x.shape[0]
    row = lambda i: (i, 0)
    const2 = lambda i: (0, 0)
    kernel = functools.partial(_merge_kernel, final=final)
    return pl.pallas_call(
        kernel,
        grid=(t // tm,),
        in_specs=[
            pl.BlockSpec((tm, BRANCH_WIDTH), row),
            pl.BlockSpec((tm, BRANCH_WIDTH), row),
            pl.BlockSpec((tm, BRANCH_WIDTH), row),
            pl.BlockSpec((tm, D_MODEL), lambda i: (i, COL_GL_2048)),
            pl.BlockSpec((tm, D_MODEL), lambda i: (i, COL_GL_2048 + 1)),
            pl.BlockSpec((tm, D_MODEL), lambda i: (i, COL_GL_2048 + 2)),
            pl.BlockSpec((tm, D_MODEL), row),
            pl.BlockSpec((None, N_BRANCH, BRANCH_WIDTH, D_MODEL), lambda i: (layer, 0, 0, 0),
                         pipeline_mode=pl.Buffered(1)),
            pl.BlockSpec((None, D_MODEL, D_MODEL), lambda i: (layer, 0, 0),
                         pipeline_mode=pl.Buffered(1)),
            pl.BlockSpec((1, D_MODEL), const2),
        ],
        out_specs=pl.BlockSpec((tm, D_MODEL), row),
        out_shape=jax.ShapeDtypeStruct((t, D_MODEL), F32),
        compiler_params=pltpu.CompilerParams(
            dimension_semantics=("arbitrary",),
            vmem_limit_bytes=VMEM_LIMIT_BYTES),
        name="merge_out",
    )(oa, ob, oc, p, p, p, x, wb, wo, final_g.reshape(1, D_MODEL))


def kernel(x, mem, norm_g, w_in, diff_lambda, diff_subln_g, hgrn_lb_raw, hgrn_norm_g,
           mem_norm_g, w_kv_mem, w_branch, w_out, final_norm_g):
    batch, seq, d = x.shape
    depth = w_in.shape[0]
    xt = x.reshape(batch * seq, d)
    memt = mem.reshape(batch * N_MEM, d)
    w_br_b = w_branch.astype(BF16)
    w_out_b = w_out.astype(BF16)
    lb_raw = hgrn_lb_raw.astype(F32)
    for l in range(depth):
        lam_init = 0.8 - 0.6 * math.exp(-0.3 * l)
        p = _norm_proj(xt, norm_g[l], w_in, layer=l, tm=1024, tn=1024, tn_rest=2048)
        kv = _norm_proj(memt, mem_norm_g[l], w_kv_mem, layer=l, tm=1024, tn=1024)
        oa = _diff_attn(p, diff_lambda[l], diff_subln_g[l], batch=batch, seq=seq,
                        lam_init=lam_init, tq=512, tk=256)
        ob = _hgrn(p, lb_raw, hgrn_norm_g[l], batch=batch, seq=seq, layer=l, rows=256)
        oc = _xattn(p, kv, batch=batch, seq=seq, tq=512)
        xt = _merge(oa, ob, oc, p, xt, w_br_b, w_out_b, final_norm_g,
                    layer=l, tm=256, final=(l == depth - 1))
    return xt.reshape(batch, seq, d)
```

```python
import functools
import math

import jax
import jax.numpy as jnp
import numpy as np
from jax import lax
from jax.experimental import pallas as pl
from jax.experimental.pallas import tpu as pltpu

F32 = jnp.float32
BF16 = jnp.bfloat16

D_MODEL = 2048
N_MEM = 256
A_HEADS = 8
A_DQK = 64
A_DV = 128
B_HEADS = 8
B_DK = 128
C_HEADS = 4
C_DH = 256
BRANCH_WIDTH = 1024
N_BRANCH = 3
EPS = 1e-6

COL_QA, COL_KA, COL_VA, COL_GA = 0, 8, 16, 24
COL_FB, COL_IB, COL_QB, COL_GB = 32, 40, 48, 56
COL_QC_1024, COL_GC_1024 = 8, 9
COL_GL_2048 = 5

VMEM_LIMIT_BYTES = 56 * 1024 * 1024

TILES = dict(
    proj_rows=1024,
    proj_cols_f32=1024,
    proj_cols_bf16=2048,
    attn_q=512,
    attn_kv=256,
    hgrn_rows=256,
    xattn_rows=512,
    merge_rows=256,
)

BF16_SUBLANES = 16
CHUNK = 64
MAX_CHUNK_DECAY = 120.0
SUB = 16


def _nt_dot(a, b):
    return lax.dot_general(a, b, (((1,), (1,)), ((), ())), preferred_element_type=F32)


def _tn_dot(a, b):
    return lax.dot_general(a, b, (((0,), (0,)), ((), ())), preferred_element_type=F32)


def _norm_proj_kernel(x_ref, g_ref, w_ref, *refs, emit_bf16_w, aliased_out):
    refs = refs[1:] if aliased_out else refs
    o_ref, h_ref = refs[0], refs[-1]

    @pl.when(pl.program_id(1) == 0)
    def _():
        x = x_ref[...]
        ms = jnp.mean(x * x, axis=-1, keepdims=True)
        h_ref[...] = (x * lax.rsqrt(ms + EPS) * g_ref[...]).astype(BF16)

    w = w_ref[...].astype(BF16)
    if emit_bf16_w:
        refs[1][...] = w
    o_ref[...] = jnp.dot(h_ref[...], w, preferred_element_type=F32).astype(o_ref.dtype)


def _norm_proj_call(kernel_kwargs, grid, in_specs, out_specs, out_shape, tm, k, **call_kwargs):
    return pl.pallas_call(
        functools.partial(_norm_proj_kernel, **kernel_kwargs),
        grid=grid, in_specs=in_specs, out_specs=out_specs, out_shape=out_shape,
        scratch_shapes=[pltpu.VMEM((tm, k), BF16)],
        compiler_params=pltpu.CompilerParams(
            dimension_semantics=("arbitrary", "arbitrary"),
            vmem_limit_bytes=VMEM_LIMIT_BYTES),
        name="norm_proj", **call_kwargs)


def _norm_proj(x, g, w, *, layer, tm, tn, tn_rest=None):
    m, k = x.shape
    n = w.shape[2]
    g = g.reshape(1, k)
    tn_rest = tn_rest or tn
    x_spec = lambda off: pl.BlockSpec((tm, k), lambda i, j: (i + off, 0))
    g_spec = pl.BlockSpec((1, k), lambda i, j: (0, 0))
    o_spec = lambda off, tn=tn: pl.BlockSpec((tm, tn), lambda i, j: (i + off, j))
    out = jax.ShapeDtypeStruct((m, n), BF16)
    first_only = m == tm
    res = _norm_proj_call(
        dict(emit_bf16_w=not first_only, aliased_out=False), (1, n // tn),
        [x_spec(0), g_spec, pl.BlockSpec((None, k, tn), lambda i, j: (layer, 0, j))],
        o_spec(0) if first_only else [o_spec(0), pl.BlockSpec((k, tn), lambda i, j: (0, j))],
        out if first_only else [out, jax.ShapeDtypeStruct((k, n), BF16)], tm, k,
    )(x, g, w)
    if first_only:
        return res
    out_head, w_bf16 = res
    return _norm_proj_call(
        dict(emit_bf16_w=False, aliased_out=True), (m // tm - 1, n // tn_rest),
        [x_spec(1), g_spec, pl.BlockSpec((k, tn_rest), lambda i, j: (0, j)),
         pl.BlockSpec(memory_space=pl.ANY)],
        o_spec(1, tn_rest), out, tm, k, input_output_aliases={3: 0},
    )(x, g, w_bf16, out_head)


def _attn_schedule(nq, nd):
    chains = []
    for c in range(nq // 2):
        own = (c, nq - 1 - c)
        steps = [(i, i * nd + d, s) for s, i in enumerate(own) for d in range(nd)]
        steps += [(i, j, s) for s, i in enumerate(own) for j in range(i * nd)]
        chains.append(steps)
    assert len({len(ch) for ch in chains}) == 1
    return np.asarray(chains, np.int32).transpose(1, 0, 2)


def _diff_attn_kernel(tab_ref, q_ref, k_ref, v_ref, gate_ref, lam_ref, g_ref, o_ref,
                      vt_ref, qt0_ref, qt1_ref, s0_ref, s1_ref, m_ref, acc_ref,
                      *, tq, tk, seq, lam_init):
    nq, nd = seq // tq, tq // tk
    n_chains = nq // 2
    qt_refs = (qt0_ref, qt1_ref)

    for c in range(seq // tk):
        vt_ref[:A_DV, c * tk:(c + 1) * tk] = v_ref[c * tk:(c + 1) * tk, :].T
    vt_ref[A_DV:, :] = jnp.ones((vt_ref.shape[0] - A_DV, seq), BF16)

    qscale = math.log2(math.e) * A_DQK ** -0.5
    sub = lax.broadcasted_iota(jnp.int32, (2 * A_DQK, tq), 0)
    for c in range(nq):
        qt = (q_ref[c * tq:(c + 1) * tq, :].astype(F32) * qscale).astype(BF16).T
        zero = jnp.zeros_like(qt)
        qt0_ref[:, c * tq:(c + 1) * tq] = jnp.where(sub < A_DQK, qt, zero)
        qt1_ref[:, c * tq:(c + 1) * tq] = jnp.where(sub >= A_DQK, qt, zero)

    m_ref[...] = jnp.full(m_ref.shape, -jnp.inf, F32)
    acc_ref[...] = jnp.zeros(acc_ref.shape, F32)

    s_refs = (s0_ref, s1_ref)

    assert nd == 2
    right = slice(tk, tq)

    def scores(t, par, diag=None):
        if diag is not None:
            causal = (lax.broadcasted_iota(jnp.int32, (tk, tk), 0)
                      <= lax.broadcasted_iota(jnp.int32, (tk, tk), 1))
        for c in range(n_chains):
            q0 = pl.multiple_of(tab_ref[t, c, 0] * tq, tq)
            k0 = pl.multiple_of(tab_ref[t, c, 1] * tk, tk)
            kb = k_ref[pl.ds(k0, tk), :]
            for mp in range(2):
                if diag is None:
                    s_refs[par][c, mp] = jnp.dot(kb, qt_refs[mp][:, pl.ds(q0, tq)],
                                                 preferred_element_type=F32)
                    continue
                ql = qt_refs[mp][:, pl.ds(q0, tk)]
                qr = qt_refs[mp][:, pl.ds(pl.multiple_of(q0 + tk, tk), tk)]
                if diag == 0:
                    s_refs[par][c, mp, :, :tk] = jnp.where(
                        causal, jnp.dot(kb, ql, preferred_element_type=F32), -jnp.inf)
                    s_refs[par][c, mp, :, right] = jnp.dot(kb, qr, preferred_element_type=F32)
                else:
                    s_refs[par][c, mp, :, right] = jnp.where(
                        causal, jnp.dot(kb, qr, preferred_element_type=F32), -jnp.inf)

    def update(t, par, cols=slice(None)):
        for c in range(n_chains):
            own = tab_ref[t, c, 2]
            k0 = pl.multiple_of(tab_ref[t, c, 1] * tk, tk)
            vtb = vt_ref[:, pl.ds(k0, tk)]
            for mp in range(2):
                s = s_refs[par][c, mp, :, cols]
                m_old = m_ref[c, own, mp, :, cols]
                m_new = jnp.maximum(m_old, jnp.max(s, axis=0, keepdims=True))
                p = jnp.exp2(s - m_new).astype(BF16)
                pv = jnp.dot(vtb, p, preferred_element_type=F32)
                acc_ref[c, own, mp, :, cols] = (
                    jnp.exp2(m_old - m_new) * acc_ref[c, own, mp, :, cols] + pv)
                m_ref[c, own, mp, :, cols] = m_new

    n_diag = 2 * nd
    n_trips = tab_ref.shape[0]
    assert (n_trips - n_diag) % 2 == 0
    scores(0, 0, diag=0)
    for t in range(n_diag):
        nxt = t + 1
        scores(nxt, nxt % 2, diag=(nxt % nd if nxt < n_diag else None))
        update(t, t % 2, cols=(right if t % nd == 1 else slice(None)))

    unroll = 4
    assert (n_trips - n_diag - 2) % unroll == 0

    def trips(u, carry):
        for v in range(unroll):
            t = n_diag + unroll * u + v
            par = (n_diag + v) % 2
            scores(t + 1, 1 - par)
            update(t, par)
        return carry

    lax.fori_loop(0, (n_trips - n_diag - 2) // unroll, trips, 0)
    scores(n_trips - 1, (n_trips - 1) % 2)
    update(n_trips - 2, (n_trips - 2) % 2)
    update(n_trips - 1, (n_trips - 1) % 2)

    lp = lam_ref[...]
    lam = (jnp.exp(jnp.sum(lp[0:1] * lp[1:2], axis=-1, keepdims=True))
           - jnp.exp(jnp.sum(lp[2:3] * lp[3:4], axis=-1, keepdims=True)) + lam_init)
    for c in range(n_chains):
        for own, qb in enumerate((c, nq - 1 - c)):
            inv_l1 = 1.0 / acc_ref[c, own, 0, A_DV:A_DV + 1]
            inv_l2 = lam / acc_ref[c, own, 1, A_DV:A_DV + 1]
            ot = acc_ref[c, own, 0, :A_DV] * inv_l1 - acc_ref[c, own, 1, :A_DV] * inv_l2
            ms = jnp.mean(ot * ot, axis=0, keepdims=True)
            ot = ot * (lax.rsqrt(ms + EPS) * (1.0 - lam_init)) * g_ref[...]
            rows = slice(qb * tq, (qb + 1) * tq)
            o_ref[rows, :] = (ot.T * jax.nn.silu(gate_ref[rows, :].astype(F32))
                              ).astype(o_ref.dtype)


def _diff_attn(p, lam_p, subln_g, *, batch, seq, lam_init, tq, tk):
    t = batch * seq
    nq, nd = seq // tq, tq // tk
    assert nq % 2 == 0
    table = _attn_schedule(nq, nd)
    kernel = functools.partial(_diff_attn_kernel, tq=tq, tk=tk, seq=seq, lam_init=lam_init)
    acc_rows = A_DV + BF16_SUBLANES
    return pl.pallas_call(
        kernel,
        grid=(batch, A_HEADS),
        in_specs=[
            pl.BlockSpec(memory_space=pltpu.SMEM),
            pl.BlockSpec((seq, 128), lambda b, h: (b, COL_QA + h)),
            pl.BlockSpec((seq, 128), lambda b, h: (b, COL_KA + h)),
            pl.BlockSpec((seq, 128), lambda b, h: (b, COL_VA + h)),
            pl.BlockSpec((seq, 128), lambda b, h: (b, COL_GA + h)),
            pl.BlockSpec((4, A_DQK), lambda b, h: (0, 0)),
            pl.BlockSpec((A_DV, 1), lambda b, h: (0, 0)),
        ],
        out_specs=pl.BlockSpec((seq, 128), lambda b, h: (b, h)),
        out_shape=jax.ShapeDtypeStruct((t, BRANCH_WIDTH), BF16),
        scratch_shapes=[
            pltpu.VMEM((acc_rows, seq), BF16),
            pltpu.VMEM((2 * A_DQK, seq), BF16),
            pltpu.VMEM((2 * A_DQK, seq), BF16),
            pltpu.VMEM((nq // 2, 2, tk, tq), F32),
            pltpu.VMEM((nq // 2, 2, tk, tq), F32),
            pltpu.VMEM((nq // 2, 2, 2, 1, tq), F32),
            pltpu.VMEM((nq // 2, 2, 2, acc_rows, tq), F32),
        ],
        compiler_params=pltpu.CompilerParams(
            dimension_semantics=("arbitrary", "arbitrary"),
            vmem_limit_bytes=VMEM_LIMIT_BYTES),
        name="diff_attn",
    )(jnp.asarray(table), p, p, p, p, lam_p, subln_g.reshape(A_DV, 1))


def _block_cumsum(x, rows, blk):
    r = lax.broadcasted_iota(jnp.int32, (rows, rows), 0)
    c = lax.broadcasted_iota(jnp.int32, (rows, rows), 1)
    tri = jnp.where((r // blk == c // blk) & (c <= r), 1.0, 0.0).astype(BF16)
    hi = x.astype(BF16)
    rem = x - hi.astype(F32)
    mid = rem.astype(BF16)
    lo = (rem - mid.astype(F32)).astype(BF16)
    return (jnp.dot(tri, hi, preferred_element_type=F32)
            + jnp.dot(tri, mid, preferred_element_type=F32)
            + jnp.dot(tri, lo, preferred_element_type=F32))


def _hgrn_kernel(f_ref, i_ref, q_ref, gate_ref, lbraw_ref, ng_ref, o_ref,
                 st_ref, b_s, q_s, lf_s, k_s, o_s, *, layer, rows):
    @pl.when(pl.program_id(1) == 0)
    def _():
        st_ref[...] = jnp.zeros(st_ref.shape, F32)

    raw = lbraw_ref[...]
    e = jnp.exp(raw - jnp.max(raw, axis=0, keepdims=True))
    w = e / jnp.sum(e, axis=0, keepdims=True)
    lb = jnp.zeros((1, raw.shape[1]), F32)
    for j in range(1, layer + 1):
        lb = lb + w[j:j + 1]

    z = f_ref[...].astype(F32)
    e = jnp.exp(-jnp.abs(z))
    inv = 1.0 / (1.0 + e)
    log_sig = jnp.minimum(z, 0.0) + jnp.log(inv)
    k_s[...] = (1.0 - lb) * (jnp.where(z >= 0.0, e, 1.0) * inv)
    a = jnp.log(lb)
    c = jnp.log1p(-lb) + log_sig
    logf = jnp.maximum(a, c) + jnp.log(1.0 + jnp.exp(-jnp.abs(a - c)))
    qf = q_ref[...].astype(F32)
    q_s[...] = qf / (1.0 + jnp.exp(-qf))
    lf_s[...] = logf

    b_s[...] = _block_cumsum(logf, rows, CHUNK)
    totals = jnp.concatenate(
        [b_s[c * CHUNK + CHUNK - 1:(c + 1) * CHUNK, :] for c in range(rows // CHUNK)], axis=0)
    fast = jnp.max(-totals) <= MAX_CHUNK_DECAY

    @pl.when(fast)
    def _():
        t_i = lax.broadcasted_iota(jnp.int32, (CHUNK, CHUNK), 0)
        s_i = lax.broadcasted_iota(jnp.int32, (CHUNK, CHUNK), 1)
        causal = s_i <= t_i
        for c in range(rows // CHUNK):
            rs = slice(c * CHUNK, (c + 1) * CHUNK)
            weights, inter = [], []
            for h in range(B_HEADS):
                cs = slice(h * B_DK, (h + 1) * B_DK)
                bn, qn, kn, vn = b_s[rs, cs], q_s[rs, cs], k_s[rs, cs], i_ref[rs, cs]
                bl = bn[CHUNK - 1:CHUNK]
                half = 0.5 * bl
                qp = (qn * jnp.exp(bn - half)).astype(BF16)
                kp = (kn * jnp.exp(half - bn)).astype(BF16)
                weights.append(_nt_dot(qp, kp))
                st = st_ref[h]
                qe = (qn * jnp.exp(bn)).astype(BF16)
                inter.append(_nt_dot(qe, st.astype(BF16)))
                kd = (kn * jnp.exp(bl - bn)).astype(BF16)
                st_ref[h] = st * jnp.exp(bl) + _tn_dot(vn, kd)
            for h in range(B_HEADS):
                cs = slice(h * B_DK, (h + 1) * B_DK)
                a = jnp.where(causal, weights[h], 0.0).astype(BF16)
                o_s[rs, cs] = inter[h] + jnp.dot(a, i_ref[rs, cs], preferred_element_type=F32)

    @pl.when(jnp.logical_not(fast))
    def _():
        b_s[...] = _block_cumsum(lf_s[...], rows, SUB)
        t_idx = lax.broadcasted_iota(jnp.int32, (SUB, 1), 0)

        def body(n, carry):
            r0 = pl.multiple_of(n * SUB, SUB)
            for h in range(B_HEADS):
                cs = slice(h * B_DK, (h + 1) * B_DK)
                bn = b_s[pl.ds(r0, SUB), cs]
                qn = q_s[pl.ds(r0, SUB), cs]
                kn = k_s[pl.ds(r0, SUB), cs]
                vn = i_ref[pl.ds(r0, SUB), cs]
                vf = vn.astype(F32)
                st = st_ref[h]
                acc = _nt_dot((qn * jnp.exp(bn)).astype(BF16), st.astype(BF16))
                for s in range(SUB):
                    arg = jnp.where(t_idx >= s, bn - bn[s:s + 1], -jnp.inf)
                    wts = qn * (kn[s:s + 1] * jnp.exp(arg))
                    acc = acc + jnp.sum(wts, axis=-1, keepdims=True) * vf[s:s + 1]
                o_s[pl.ds(r0, SUB), cs] = acc
                bl = bn[SUB - 1:SUB]
                kd = kn * jnp.exp(bl - bn)
                st_ref[h] = st * jnp.exp(bl) + _tn_dot(vn, kd.astype(BF16))
            return carry

        lax.fori_loop(0, rows // SUB, body, 0)

    for h in range(B_HEADS):
        cs = slice(h * B_DK, (h + 1) * B_DK)
        o = o_s[:, cs]
        ms = jnp.mean(o * o, axis=-1, keepdims=True)
        y = o * lax.rsqrt(ms + EPS) * ng_ref[...]
        o_ref[:, cs] = (y * jax.nn.silu(gate_ref[:, cs].astype(F32))).astype(o_ref.dtype)


def _hgrn(p, lb_raw, norm_g, *, batch, seq, layer, rows):
    t = batch * seq
    nr = seq // rows
    kernel = functools.partial(_hgrn_kernel, layer=layer, rows=rows)
    blk = lambda col: pl.BlockSpec((rows, 1024), lambda b, i: (b * nr + i, col // 8))
    depth = lb_raw.shape[0]
    return pl.pallas_call(
        kernel,
        grid=(batch, nr),
        in_specs=[
            blk(COL_FB), blk(COL_IB), blk(COL_QB), blk(COL_GB),
            pl.BlockSpec((depth, 1024), lambda b, i: (0, 0)),
            pl.BlockSpec((1, B_DK), lambda b, i: (0, 0)),
        ],
        out_specs=pl.BlockSpec((rows, 1024), lambda b, i: (b * nr + i, 0)),
        out_shape=jax.ShapeDtypeStruct((t, BRANCH_WIDTH), BF16),
        scratch_shapes=[
            pltpu.VMEM((B_HEADS, B_DK, B_DK), F32),
            pltpu.VMEM((rows, 1024), F32),
            pltpu.VMEM((rows, 1024), F32),
            pltpu.VMEM((rows, 1024), F32),
            pltpu.VMEM((rows, 1024), F32),
            pltpu.VMEM((rows, 1024), F32),
        ],
        compiler_params=pltpu.CompilerParams(
            dimension_semantics=("arbitrary", "arbitrary"),
            vmem_limit_bytes=VMEM_LIMIT_BYTES),
        name="hgrn2",
    )(p, p, p, p, lb_raw, norm_g.reshape(1, B_DK))


def _xattn_kernel(q_ref, gate_ref, k_ref, v_ref, o_ref):
    heads = [slice(h * C_DH, (h + 1) * C_DH) for h in range(C_HEADS)]
    scores = [_nt_dot(q_ref[:, cs], k_ref[:, cs]) for cs in heads]
    c = (C_DH ** -0.5) * math.log2(math.e)
    for cs, s in zip(heads, scores):
        m = jnp.max(s, axis=-1, keepdims=True)
        pr = jnp.exp2((s - m) * c)
        inv_l = 1.0 / jnp.sum(pr, axis=-1, keepdims=True)
        o = jnp.dot(pr.astype(BF16), v_ref[:, cs], preferred_element_type=F32) * inv_l
        o_ref[:, cs] = (o * jax.nn.silu(gate_ref[:, cs].astype(F32))).astype(o_ref.dtype)


def _xattn(p, kv, *, batch, seq, tq):
    t = batch * seq
    nq = seq // tq
    return pl.pallas_call(
        _xattn_kernel,
        grid=(batch, nq),
        in_specs=[
            pl.BlockSpec((tq, 1024), lambda b, i: (b * nq + i, COL_QC_1024)),
            pl.BlockSpec((tq, 1024), lambda b, i: (b * nq + i, COL_GC_1024)),
            pl.BlockSpec((N_MEM, 1024), lambda b, i: (b, 0)),
            pl.BlockSpec((N_MEM, 1024), lambda b, i: (b, 1)),
        ],
        out_specs=pl.BlockSpec((tq, 1024), lambda b, i: (b * nq + i, 0)),
        out_shape=jax.ShapeDtypeStruct((t, BRANCH_WIDTH), BF16),
        compiler_params=pltpu.CompilerParams(
            dimension_semantics=("arbitrary", "arbitrary"),
            vmem_limit_bytes=VMEM_LIMIT_BYTES),
        name="mem_xattn",
    )(p, p, kv, kv)


def _merge_kernel(oa_ref, ob_ref, oc_ref, g0_ref, g1_ref, g2_ref, x_ref, wb_ref, wo_ref,
                  fg_ref, o_ref, *, final):
    half = D_MODEL // 2
    ys = []
    for cols in (slice(0, half), slice(half, D_MODEL)):
        y = None
        for o_r, g_r, j in ((oa_ref, g0_ref, 0), (ob_ref, g1_ref, 1), (oc_ref, g2_ref, 2)):
            term = jax.nn.sigmoid(g_r[:, cols].astype(F32)) * jnp.dot(
                o_r[...], wb_ref[j, :, cols], preferred_element_type=F32)
            y = term if y is None else y + term
        ys.append(y.astype(BF16))
    xn = x_ref[...] + jnp.dot(ys[0], wo_ref[:half, :], preferred_element_type=F32)
    xn = xn + jnp.dot(ys[1], wo_ref[half:, :], preferred_element_type=F32)
    if final:
        ms = jnp.mean(xn * xn, axis=-1, keepdims=True)
        xn = xn * lax.rsqrt(ms + EPS) * fg_ref[...]
    o_ref[...] = xn


def _merge(oa, ob, oc, p, x, wb, wo, final_g, *, layer, tm, final):
    t = x.shape[0]
    row = lambda i: (i, 0)
    const2 = lambda i: (0, 0)
    kernel = functools.partial(_merge_kernel, final=final)
    return pl.pallas_call(
        kernel,
        grid=(t // tm,),
        in_specs=[
            pl.BlockSpec((tm, BRANCH_WIDTH), row),
            pl.BlockSpec((tm, BRANCH_WIDTH), row),
            pl.BlockSpec((tm, BRANCH_WIDTH), row),
            pl.BlockSpec((tm, D_MODEL), lambda i: (i, COL_GL_2048)),
            pl.BlockSpec((tm, D_MODEL), lambda i: (i, COL_GL_2048 + 1)),
            pl.BlockSpec((tm, D_MODEL), lambda i: (i, COL_GL_2048 + 2)),
            pl.BlockSpec((tm, D_MODEL), row),
            pl.BlockSpec((None, N_BRANCH, BRANCH_WIDTH, D_MODEL), lambda i: (layer, 0, 0, 0),
                         pipeline_mode=pl.Buffered(1)),
            pl.BlockSpec((None, D_MODEL, D_MODEL), lambda i: (layer, 0, 0),
                         pipeline_mode=pl.Buffered(1)),
            pl.BlockSpec((1, D_MODEL), const2),
        ],
        out_specs=pl.BlockSpec((tm, D_MODEL), row),
        out_shape=jax.ShapeDtypeStruct((t, D_MODEL), F32),
        compiler_params=pltpu.CompilerParams(
            dimension_semantics=("arbitrary",),
            vmem_limit_bytes=VMEM_LIMIT_BYTES),
        name="merge_out",
    )(oa, ob, oc, p, p, p, x, wb, wo, final_g.reshape(1, D_MODEL))


def kernel(x, mem, norm_g, w_in, diff_lambda, diff_subln_g, hgrn_lb_raw, hgrn_norm_g,
           mem_norm_g, w_kv_mem, w_branch, w_out, final_norm_g):
    batch, seq, d = x.shape
    depth = w_in.shape[0]
    xt = x.reshape(batch * seq, d)
    memt = mem.reshape(batch * N_MEM, d)
    w_br_b = w_branch.astype(BF16)
    w_out_b = w_out.astype(BF16)
    lb_raw = hgrn_lb_raw.astype(F32)
    t = TILES
    for l in range(depth):
        lam_init = 0.8 - 0.6 * math.exp(-0.3 * l)
        p = _norm_proj(xt, norm_g[l], w_in, layer=l, tm=t["proj_rows"],
                       tn=t["proj_cols_f32"], tn_rest=t["proj_cols_bf16"])
        kv = _norm_proj(memt, mem_norm_g[l], w_kv_mem, layer=l, tm=t["proj_rows"],
                        tn=t["proj_cols_f32"])
        oa = _diff_attn(p, diff_lambda[l], diff_subln_g[l], batch=batch, seq=seq,
                        lam_init=lam_init, tq=t["attn_q"], tk=t["attn_kv"])
        ob = _hgrn(p, lb_raw, hgrn_norm_g[l], batch=batch, seq=seq, layer=l,
                   rows=t["hgrn_rows"])
        oc = _xattn(p, kv, batch=batch, seq=seq, tq=t["xattn_rows"])
        xt = _merge(oa, ob, oc, p, xt, w_br_b, w_out_b, final_norm_g,
                    layer=l, tm=t["merge_rows"], final=(l == depth - 1))
    return xt.reshape(batch, seq, d)
```

```python
import functools
import math

import jax
import jax.numpy as jnp
import numpy as np
from jax import lax
from jax.experimental import pallas as pl
from jax.experimental.pallas import tpu as pltpu

F32 = jnp.float32
BF16 = jnp.bfloat16

D_MODEL = 2048
N_MEM = 256
A_HEADS = 8
A_DQK = 64
A_DV = 128
B_HEADS = 8
B_DK = 128
C_HEADS = 4
C_DH = 256
BRANCH_WIDTH = 1024
N_BRANCH = 3
EPS = 1e-6

ATTN_COLS = 4096
SLAB_Q, SLAB_K, SLAB_V, SLAB_G = 0, 1, 2, 3
COL_FB, COL_IB, COL_QB, COL_GB = 0, 8, 16, 24
COL_QC_1024, COL_GC_1024 = 4, 5
COL_GL_2048 = 3

VMEM_LIMIT_BYTES = 56 * 1024 * 1024
PROJ_VMEM_LIMIT_BYTES = 62 * 1024 * 1024

TILES = dict(
    proj_rows=1024,
    proj_cols_f32=512,
    proj_cols_bf16=2048,
    kv_cols=1024,
    attn_q=512,
    attn_kv=256,
    hgrn_rows=256,
    xattn_rows=512,
    merge_rows=256,
)

BF16_SUBLANES = 16
CHUNK = 64
MAX_CHUNK_DECAY = 120.0
SUB = 16


def _nt_dot(a, b):
    return lax.dot_general(a, b, (((1,), (1,)), ((), ())), preferred_element_type=F32)


def _tn_dot(a, b):
    return lax.dot_general(a, b, (((0,), (0,)), ((), ())), preferred_element_type=F32)


def _norm_proj_kernel(x_ref, g_ref, w_ref, *refs, n_aliased, n_slab_tiles, emit_bf16_w):
    refs = refs[n_aliased:]
    o_ref, h_ref = refs[0], refs[-1]
    j = pl.program_id(1)

    @pl.when(j == 0)
    def _():
        x = x_ref[...]
        ms = jnp.mean(x * x, axis=-1, keepdims=True)
        h_ref[...] = (x * lax.rsqrt(ms + EPS) * g_ref[...]).astype(BF16)
        if n_slab_tiles:
            refs[1][...] = jnp.zeros(refs[1].shape, refs[1].dtype)

    w = w_ref[...].astype(BF16)
    if emit_bf16_w:
        refs[2][...] = w
    res = jnp.dot(h_ref[...], w, preferred_element_type=F32).astype(o_ref.dtype)
    if n_slab_tiles == 0:
        o_ref[...] = res
        return
    slab_ref = refs[1]

    o_ref[...] = res
    is_slab_tile = j < n_slab_tiles
    for s in range(slab_ref.shape[0]):
        slab_ref[s] = jnp.where(is_slab_tile, res[:, s * 128:(s + 1) * 128], slab_ref[s])


def _norm_proj_call(kernel_kwargs, grid, in_specs, out_specs, out_shape, tm, k, **call_kwargs):
    return pl.pallas_call(
        functools.partial(_norm_proj_kernel, **kernel_kwargs),
        grid=grid, in_specs=in_specs, out_specs=out_specs, out_shape=out_shape,
        scratch_shapes=[pltpu.VMEM((tm, k), BF16)],
        compiler_params=pltpu.CompilerParams(
            dimension_semantics=("arbitrary", "arbitrary"),
            vmem_limit_bytes=PROJ_VMEM_LIMIT_BYTES),
        name="norm_proj", **call_kwargs)


def _norm_proj(x, g, w, *, layer, tm, tn):
    m, k = x.shape
    n = w.shape[2]
    assert m == tm
    return _norm_proj_call(
        dict(n_aliased=0, n_slab_tiles=0, emit_bf16_w=False), (1, n // tn),
        [pl.BlockSpec((tm, k), lambda i, j: (i, 0)),
         pl.BlockSpec((1, k), lambda i, j: (0, 0)),
         pl.BlockSpec((None, k, tn), lambda i, j: (layer, 0, j))],
        pl.BlockSpec((tm, tn), lambda i, j: (i, j)),
        jax.ShapeDtypeStruct((m, n), BF16), tm, k,
    )(x, g.reshape(1, k), w)


def _in_proj(x, g, w, *, layer, tm, tn, tn_rest):
    m, k = x.shape
    n = w.shape[2]
    g = g.reshape(1, k)
    outs = [jax.ShapeDtypeStruct((m, n - ATTN_COLS), BF16),
            jax.ShapeDtypeStruct((ATTN_COLS // 128, m, 128), BF16)]
    x_spec = lambda off: pl.BlockSpec((tm, k), lambda i, j: (i + off, 0))
    g_spec = pl.BlockSpec((1, k), lambda i, j: (0, 0))

    def out_specs(off, tn):
        ns = ATTN_COLS // tn
        return ns, [
            pl.BlockSpec((tm, tn), lambda i, j: (i + off, jnp.maximum(j - ns, 0))),
            pl.BlockSpec((tn // 128, tm, 128), lambda i, j: (jnp.minimum(j, ns - 1), i + off, 0)),
        ]

    ns, specs = out_specs(0, tn)
    p_head, slab_head, w_bf16 = _norm_proj_call(
        dict(n_aliased=0, n_slab_tiles=ns, emit_bf16_w=True), (1, n // tn),
        [x_spec(0), g_spec, pl.BlockSpec((None, k, tn), lambda i, j: (layer, 0, j))],
        specs + [pl.BlockSpec((k, tn), lambda i, j: (0, j))],
        outs + [jax.ShapeDtypeStruct((k, n), BF16)], tm, k,
    )(x, g, w)
    ns, specs = out_specs(1, tn_rest)
    any_spec = pl.BlockSpec(memory_space=pl.ANY)
    return _norm_proj_call(
        dict(n_aliased=2, n_slab_tiles=ns, emit_bf16_w=False), (m // tm - 1, n // tn_rest),
        [x_spec(1), g_spec, pl.BlockSpec((k, tn_rest), lambda i, j: (0, j)), any_spec, any_spec],
        specs, outs, tm, k, input_output_aliases={3: 0, 4: 1},
    )(x, g, w_bf16, p_head, slab_head)


def _attn_schedule(nq, nd):
    chains = []
    for c in range(nq // 2):
        own = (c, nq - 1 - c)
        steps = [(i, i * nd + d, s) for s, i in enumerate(own) for d in range(nd)]
        steps += [(i, j, s) for s, i in enumerate(own) for j in range(i * nd)]
        chains.append(steps)
    assert len({len(ch) for ch in chains}) == 1
    return np.asarray(chains, np.int32).transpose(1, 0, 2)


def _diff_attn_kernel(tab_ref, q_ref, k_ref, v_ref, gate_ref, lam_ref, g_ref, o_ref,
                      vt_ref, qt0_ref, qt1_ref, s0_ref, s1_ref, m_ref, acc_ref,
                      *, tq, tk, seq, lam_init):
    nq, nd = seq // tq, tq // tk
    n_chains = nq // 2
    qt_refs = (qt0_ref, qt1_ref)

    for c in range(seq // tk):
        vt_ref[:A_DV, c * tk:(c + 1) * tk] = v_ref[c * tk:(c + 1) * tk, :].T
    vt_ref[A_DV:, :] = jnp.ones((vt_ref.shape[0] - A_DV, seq), BF16)

    qscale = math.log2(math.e) * A_DQK ** -0.5
    sub = lax.broadcasted_iota(jnp.int32, (2 * A_DQK, tq), 0)
    for c in range(nq):
        qt = (q_ref[c * tq:(c + 1) * tq, :].astype(F32) * qscale).astype(BF16).T
        zero = jnp.zeros_like(qt)
        qt0_ref[:, c * tq:(c + 1) * tq] = jnp.where(sub < A_DQK, qt, zero)
        qt1_ref[:, c * tq:(c + 1) * tq] = jnp.where(sub >= A_DQK, qt, zero)

    m_ref[...] = jnp.full(m_ref.shape, -jnp.inf, F32)
    acc_ref[...] = jnp.zeros(acc_ref.shape, F32)

    s_refs = (s0_ref, s1_ref)

    assert nd == 2
    right = slice(tk, tq)

    def scores(t, par, diag=None):
        if diag is not None:
            causal = (lax.broadcasted_iota(jnp.int32, (tk, tk), 0)
                      <= lax.broadcasted_iota(jnp.int32, (tk, tk), 1))
        for c in range(n_chains):
            q0 = pl.multiple_of(tab_ref[t, c, 0] * tq, tq)
            k0 = pl.multiple_of(tab_ref[t, c, 1] * tk, tk)
            kb = k_ref[pl.ds(k0, tk), :]
            for mp in range(2):
                if diag is None:
                    s_refs[par][c, mp] = jnp.dot(kb, qt_refs[mp][:, pl.ds(q0, tq)],
                                                 preferred_element_type=F32)
                    continue
                ql = qt_refs[mp][:, pl.ds(q0, tk)]
                qr = qt_refs[mp][:, pl.ds(pl.multiple_of(q0 + tk, tk), tk)]
                if diag == 0:
                    s_refs[par][c, mp, :, :tk] = jnp.where(
                        causal, jnp.dot(kb, ql, preferred_element_type=F32), -jnp.inf)
                    s_refs[par][c, mp, :, right] = jnp.dot(kb, qr, preferred_element_type=F32)
                else:
                    s_refs[par][c, mp, :, right] = jnp.where(
                        causal, jnp.dot(kb, qr, preferred_element_type=F32), -jnp.inf)

    def update(t, par, cols=slice(None)):
        for c in range(n_chains):
            own = tab_ref[t, c, 2]
            k0 = pl.multiple_of(tab_ref[t, c, 1] * tk, tk)
            vtb = vt_ref[:, pl.ds(k0, tk)]
            for mp in range(2):
                s = s_refs[par][c, mp, :, cols]
                m_old = m_ref[c, own, mp, :, cols]
                m_new = jnp.maximum(m_old, jnp.max(s, axis=0, keepdims=True))
                p = jnp.exp2(s - m_new).astype(BF16)
                pv = jnp.dot(vtb, p, preferred_element_type=F32)
                acc_ref[c, own, mp, :, cols] = (
                    jnp.exp2(m_old - m_new) * acc_ref[c, own, mp, :, cols] + pv)
                m_ref[c, own, mp, :, cols] = m_new

    n_diag = 2 * nd
    n_trips = tab_ref.shape[0]
    assert (n_trips - n_diag) % 2 == 0
    scores(0, 0, diag=0)
    for t in range(n_diag):
        nxt = t + 1
        scores(nxt, nxt % 2, diag=(nxt % nd if nxt < n_diag else None))
        update(t, t % 2, cols=(right if t % nd == 1 else slice(None)))

    unroll = 4
    assert (n_trips - n_diag - 2) % unroll == 0

    def trips(u, carry):
        for v in range(unroll):
            t = n_diag + unroll * u + v
            par = (n_diag + v) % 2
            scores(t + 1, 1 - par)
            update(t, par)
        return carry

    lax.fori_loop(0, (n_trips - n_diag - 2) // unroll, trips, 0)
    scores(n_trips - 1, (n_trips - 1) % 2)
    update(n_trips - 2, (n_trips - 2) % 2)
    update(n_trips - 1, (n_trips - 1) % 2)

    lp = lam_ref[...]
    lam = (jnp.exp(jnp.sum(lp[0:1] * lp[1:2], axis=-1, keepdims=True))
           - jnp.exp(jnp.sum(lp[2:3] * lp[3:4], axis=-1, keepdims=True)) + lam_init)
    for c in range(n_chains):
        for own, qb in enumerate((c, nq - 1 - c)):
            inv_l1 = 1.0 / acc_ref[c, own, 0, A_DV:A_DV + 1]
            inv_l2 = lam / acc_ref[c, own, 1, A_DV:A_DV + 1]
            ot = acc_ref[c, own, 0, :A_DV] * inv_l1 - acc_ref[c, own, 1, :A_DV] * inv_l2
            ms = jnp.mean(ot * ot, axis=0, keepdims=True)
            ot = ot * (lax.rsqrt(ms + EPS) * (1.0 - lam_init)) * g_ref[...]
            rows = slice(qb * tq, (qb + 1) * tq)
            o_ref[rows, :] = (ot.T * jax.nn.silu(gate_ref[rows, :].astype(F32))
                              ).astype(o_ref.dtype)


def _diff_attn(slabs, lam_p, subln_g, *, batch, seq, lam_init, tq, tk):
    t = batch * seq
    nq, nd = seq // tq, tq // tk
    assert nq % 2 == 0
    table = _attn_schedule(nq, nd)
    kernel = functools.partial(_diff_attn_kernel, tq=tq, tk=tk, seq=seq, lam_init=lam_init)
    acc_rows = A_DV + BF16_SUBLANES
    slab = lambda grp: pl.BlockSpec((None, seq, 128), lambda b, h: (grp * A_HEADS + h, b, 0))
    return pl.pallas_call(
        kernel,
        grid=(batch, A_HEADS),
        in_specs=[
            pl.BlockSpec(memory_space=pltpu.SMEM),
            slab(SLAB_Q), slab(SLAB_K), slab(SLAB_V), slab(SLAB_G),
            pl.BlockSpec((4, A_DQK), lambda b, h: (0, 0)),
            pl.BlockSpec((A_DV, 1), lambda b, h: (0, 0)),
        ],
        out_specs=pl.BlockSpec((seq, 128), lambda b, h: (b, h)),
        out_shape=jax.ShapeDtypeStruct((t, BRANCH_WIDTH), BF16),
        scratch_shapes=[
            pltpu.VMEM((acc_rows, seq), BF16),
            pltpu.VMEM((2 * A_DQK, seq), BF16),
            pltpu.VMEM((2 * A_DQK, seq), BF16),
            pltpu.VMEM((nq // 2, 2, tk, tq), F32),
            pltpu.VMEM((nq // 2, 2, tk, tq), F32),
            pltpu.VMEM((nq // 2, 2, 2, 1, tq), F32),
            pltpu.VMEM((nq // 2, 2, 2, acc_rows, tq), F32),
        ],
        compiler_params=pltpu.CompilerParams(
            dimension_semantics=("arbitrary", "arbitrary"),
            vmem_limit_bytes=VMEM_LIMIT_BYTES),
        name="diff_attn",
    )(jnp.asarray(table), slabs, slabs, slabs, slabs, lam_p, subln_g.reshape(A_DV, 1))


def _block_cumsum(x, rows, blk):
    r = lax.broadcasted_iota(jnp.int32, (rows, rows), 0)
    c = lax.broadcasted_iota(jnp.int32, (rows, rows), 1)
    tri = jnp.where((r // blk == c // blk) & (c <= r), 1.0, 0.0).astype(BF16)
    hi = x.astype(BF16)
    rem = x - hi.astype(F32)
    mid = rem.astype(BF16)
    lo = (rem - mid.astype(F32)).astype(BF16)
    return (jnp.dot(tri, hi, preferred_element_type=F32)
            + jnp.dot(tri, mid, preferred_element_type=F32)
            + jnp.dot(tri, lo, preferred_element_type=F32))


def _hgrn_kernel(f_ref, i_ref, q_ref, gate_ref, lbraw_ref, ng_ref, o_ref,
                 st_ref, b_s, q_s, lf_s, k_s, o_s, *, layer, rows):
    @pl.when(pl.program_id(1) == 0)
    def _():
        st_ref[...] = jnp.zeros(st_ref.shape, F32)

    raw = lbraw_ref[...]
    e = jnp.exp(raw - jnp.max(raw, axis=0, keepdims=True))
    w = e / jnp.sum(e, axis=0, keepdims=True)
    lb = jnp.zeros((1, raw.shape[1]), F32)
    for j in range(1, layer + 1):
        lb = lb + w[j:j + 1]

    z = f_ref[...].astype(F32)
    e = jnp.exp(-jnp.abs(z))
    inv = 1.0 / (1.0 + e)
    log_sig = jnp.minimum(z, 0.0) + jnp.log(inv)
    k_s[...] = (1.0 - lb) * (jnp.where(z >= 0.0, e, 1.0) * inv)
    a = jnp.log(lb)
    c = jnp.log1p(-lb) + log_sig
    logf = jnp.maximum(a, c) + jnp.log(1.0 + jnp.exp(-jnp.abs(a - c)))
    qf = q_ref[...].astype(F32)
    q_s[...] = qf / (1.0 + jnp.exp(-qf))
    lf_s[...] = logf

    b_s[...] = _block_cumsum(logf, rows, CHUNK)
    totals = jnp.concatenate(
        [b_s[c * CHUNK + CHUNK - 1:(c + 1) * CHUNK, :] for c in range(rows // CHUNK)], axis=0)
    fast = jnp.max(-totals) <= MAX_CHUNK_DECAY

    @pl.when(fast)
    def _():
        t_i = lax.broadcasted_iota(jnp.int32, (CHUNK, CHUNK), 0)
        s_i = lax.broadcasted_iota(jnp.int32, (CHUNK, CHUNK), 1)
        causal = s_i <= t_i
        for c in range(rows // CHUNK):
            rs = slice(c * CHUNK, (c + 1) * CHUNK)
            weights, inter = [], []
            for h in range(B_HEADS):
                cs = slice(h * B_DK, (h + 1) * B_DK)
                bn, qn, kn, vn = b_s[rs, cs], q_s[rs, cs], k_s[rs, cs], i_ref[rs, cs]
                bl = bn[CHUNK - 1:CHUNK]
                half = 0.5 * bl
                qp = (qn * jnp.exp(bn - half)).astype(BF16)
                kp = (kn * jnp.exp(half - bn)).astype(BF16)
                weights.append(_nt_dot(qp, kp))
                st = st_ref[h]
                qe = (qn * jnp.exp(bn)).astype(BF16)
                inter.append(_nt_dot(qe, st.astype(BF16)))
                kd = (kn * jnp.exp(bl - bn)).astype(BF16)
                st_ref[h] = st * jnp.exp(bl) + _tn_dot(vn, kd)
            for h in range(B_HEADS):
                cs = slice(h * B_DK, (h + 1) * B_DK)
                a = jnp.where(causal, weights[h], 0.0).astype(BF16)
                o_s[rs, cs] = inter[h] + jnp.dot(a, i_ref[rs, cs], preferred_element_type=F32)

    @pl.when(jnp.logical_not(fast))
    def _():
        b_s[...] = _block_cumsum(lf_s[...], rows, SUB)
        t_idx = lax.broadcasted_iota(jnp.int32, (SUB, 1), 0)

        def body(n, carry):
            r0 = pl.multiple_of(n * SUB, SUB)
            for h in range(B_HEADS):
                cs = slice(h * B_DK, (h + 1) * B_DK)
                bn = b_s[pl.ds(r0, SUB), cs]
                qn = q_s[pl.ds(r0, SUB), cs]
                kn = k_s[pl.ds(r0, SUB), cs]
                vn = i_ref[pl.ds(r0, SUB), cs]
                vf = vn.astype(F32)
                st = st_ref[h]
                acc = _nt_dot((qn * jnp.exp(bn)).astype(BF16), st.astype(BF16))
                for s in range(SUB):
                    arg = jnp.where(t_idx >= s, bn - bn[s:s + 1], -jnp.inf)
                    wts = qn * (kn[s:s + 1] * jnp.exp(arg))
                    acc = acc + jnp.sum(wts, axis=-1, keepdims=True) * vf[s:s + 1]
                o_s[pl.ds(r0, SUB), cs] = acc
                bl = bn[SUB - 1:SUB]
                kd = kn * jnp.exp(bl - bn)
                st_ref[h] = st * jnp.exp(bl) + _tn_dot(vn, kd.astype(BF16))
            return carry

        lax.fori_loop(0, rows // SUB, body, 0)

    for h in range(B_HEADS):
        cs = slice(h * B_DK, (h + 1) * B_DK)
        o = o_s[:, cs]
        ms = jnp.mean(o * o, axis=-1, keepdims=True)
        y = o * lax.rsqrt(ms + EPS) * ng_ref[...]
        o_ref[:, cs] = (y * jax.nn.silu(gate_ref[:, cs].astype(F32))).astype(o_ref.dtype)


def _hgrn(p, lb_raw, norm_g, *, batch, seq, layer, rows):
    t = batch * seq
    nr = seq // rows
    kernel = functools.partial(_hgrn_kernel, layer=layer, rows=rows)
    blk = lambda col: pl.BlockSpec((rows, 1024), lambda b, i: (b * nr + i, col // 8))
    depth = lb_raw.shape[0]
    return pl.pallas_call(
        kernel,
        grid=(batch, nr),
        in_specs=[
            blk(COL_FB), blk(COL_IB), blk(COL_QB), blk(COL_GB),
            pl.BlockSpec((depth, 1024), lambda b, i: (0, 0)),
            pl.BlockSpec((1, B_DK), lambda b, i: (0, 0)),
        ],
        out_specs=pl.BlockSpec((rows, 1024), lambda b, i: (b * nr + i, 0)),
        out_shape=jax.ShapeDtypeStruct((t, BRANCH_WIDTH), BF16),
        scratch_shapes=[
            pltpu.VMEM((B_HEADS, B_DK, B_DK), F32),
            pltpu.VMEM((rows, 1024), F32),
            pltpu.VMEM((rows, 1024), F32),
            pltpu.VMEM((rows, 1024), F32),
            pltpu.VMEM((rows, 1024), F32),
            pltpu.VMEM((rows, 1024), F32),
        ],
        compiler_params=pltpu.CompilerParams(
            dimension_semantics=("arbitrary", "arbitrary"),
            vmem_limit_bytes=VMEM_LIMIT_BYTES),
        name="hgrn2",
    )(p, p, p, p, lb_raw, norm_g.reshape(1, B_DK))


def _xattn_kernel(q_ref, gate_ref, k_ref, v_ref, o_ref):
    heads = [slice(h * C_DH, (h + 1) * C_DH) for h in range(C_HEADS)]
    scores = [_nt_dot(q_ref[:, cs], k_ref[:, cs]) for cs in heads]
    c = (C_DH ** -0.5) * math.log2(math.e)
    for cs, s in zip(heads, scores):
        m = jnp.max(s, axis=-1, keepdims=True)
        pr = jnp.exp2((s - m) * c)
        inv_l = 1.0 / jnp.sum(pr, axis=-1, keepdims=True)
        o = jnp.dot(pr.astype(BF16), v_ref[:, cs], preferred_element_type=F32) * inv_l
        o_ref[:, cs] = (o * jax.nn.silu(gate_ref[:, cs].astype(F32))).astype(o_ref.dtype)


def _xattn(p, kv, *, batch, seq, tq):
    t = batch * seq
    nq = seq // tq
    return pl.pallas_call(
        _xattn_kernel,
        grid=(batch, nq),
        in_specs=[
            pl.BlockSpec((tq, 1024), lambda b, i: (b * nq + i, COL_QC_1024)),
            pl.BlockSpec((tq, 1024), lambda b, i: (b * nq + i, COL_GC_1024)),
            pl.BlockSpec((N_MEM, 1024), lambda b, i: (b, 0)),
            pl.BlockSpec((N_MEM, 1024), lambda b, i: (b, 1)),
        ],
        out_specs=pl.BlockSpec((tq, 1024), lambda b, i: (b * nq + i, 0)),
        out_shape=jax.ShapeDtypeStruct((t, BRANCH_WIDTH), BF16),
        compiler_params=pltpu.CompilerParams(
            dimension_semantics=("arbitrary", "arbitrary"),
            vmem_limit_bytes=VMEM_LIMIT_BYTES),
        name="mem_xattn",
    )(p, p, kv, kv)


def _merge_kernel(oa_ref, ob_ref, oc_ref, g0_ref, g1_ref, g2_ref, x_ref, wb_ref, wo_ref,
                  fg_ref, o_ref, *, final):
    half = D_MODEL // 2
    ys = []
    for cols in (slice(0, half), slice(half, D_MODEL)):
        y = None
        for o_r, g_r, j in ((oa_ref, g0_ref, 0), (ob_ref, g1_ref, 1), (oc_ref, g2_ref, 2)):
            term = jax.nn.sigmoid(g_r[:, cols].astype(F32)) * jnp.dot(
                o_r[...], wb_ref[j, :, cols], preferred_element_type=F32)
            y = term if y is None else y + term
        ys.append(y.astype(BF16))
    xn = x_ref[...] + jnp.dot(ys[0], wo_ref[:half, :], preferred_element_type=F32)
    xn = xn + jnp.dot(ys[1], wo_ref[half:, :], preferred_element_type=F32)
    if final:
        ms = jnp.mean(xn * xn, axis=-1, keepdims=True)
        xn = xn * lax.rsqrt(ms + EPS) * fg_ref[...]
    o_ref[...] = xn


def _merge(oa, ob, oc, p, x, wb, wo, final_g, *, layer, tm, final):
    t = x.shape[0]
    row = lambda i: (i, 0)
    const2 = lambda i: (0, 0)
    kernel = functools.partial(_merge_kernel, final=final)
    return pl.pallas_call(
        kernel,
        grid=(t // tm,),
        in_specs=[
            pl.BlockSpec((tm, BRANCH_WIDTH), row),
            pl.BlockSpec((tm, BRANCH_WIDTH), row),
            pl.BlockSpec((tm, BRANCH_WIDTH), row),
            pl.BlockSpec((tm, D_MODEL), lambda i: (i, COL_GL_2048)),
            pl.BlockSpec((tm, D_MODEL), lambda i: (i, COL_GL_2048 + 1)),
            pl.BlockSpec((tm, D_MODEL), lambda i: (i, COL_GL_2048 + 2)),
            pl.BlockSpec((tm, D_MODEL), row),
            pl.BlockSpec((None, N_BRANCH, BRANCH_WIDTH, D_MODEL), lambda i: (layer, 0, 0, 0),
                         pipeline_mode=pl.Buffered(1)),
            pl.BlockSpec((None, D_MODEL, D_MODEL), lambda i: (layer, 0, 0),
                         pipeline_mode=pl.Buffered(1)),
            pl.BlockSpec((1, D_MODEL), const2),
        ],
        out_specs=pl.BlockSpec((tm, D_MODEL), row),
        out_shape=jax.ShapeDtypeStruct((t, D_MODEL), F32),
        compiler_params=pltpu.CompilerParams(
            dimension_semantics=("arbitrary",),
            vmem_limit_bytes=VMEM_LIMIT_BYTES),
        name="merge_out",
    )(oa, ob, oc, p, p, p, x, wb, wo, final_g.reshape(1, D_MODEL))


def kernel(x, mem, norm_g, w_in, diff_lambda, diff_subln_g, hgrn_lb_raw, hgrn_norm_g,
           mem_norm_g, w_kv_mem, w_branch, w_out, final_norm_g):
    batch, seq, d = x.shape
    depth = w_in.shape[0]
    xt = x.reshape(batch * seq, d)
    memt = mem.reshape(batch * N_MEM, d)
    w_br_b = w_branch.astype(BF16)
    w_out_b = w_out.astype(BF16)
    lb_raw = hgrn_lb_raw.astype(F32)
    t = TILES
    for l in range(depth):
        lam_init = 0.8 - 0.6 * math.exp(-0.3 * l)
        p, slabs = _in_proj(xt, norm_g[l], w_in, layer=l, tm=t["proj_rows"],
                            tn=t["proj_cols_f32"], tn_rest=t["proj_cols_bf16"])
        kv = _norm_proj(memt, mem_norm_g[l], w_kv_mem, layer=l, tm=t["proj_rows"],
                        tn=t["kv_cols"])
        oa = _diff_attn(slabs, diff_lambda[l], diff_subln_g[l], batch=batch, seq=seq,
                        lam_init=lam_init, tq=t["attn_q"], tk=t["attn_kv"])
        ob = _hgrn(p, lb_raw, hgrn_norm_g[l], batch=batch, seq=seq, layer=l,
                   rows=t["hgrn_rows"])
        oc = _xattn(p, kv, batch=batch, seq=seq, tq=t["xattn_rows"])
        xt = _merge(oa, ob, oc, p, xt, w_br_b, w_out_b, final_norm_g,
                    layer=l, tm=t["merge_rows"], final=(l == depth - 1))
    return xt.reshape(batch, seq, d)
```

```python
import functools
import math

import jax
import jax.numpy as jnp
import numpy as np
from jax import lax
from jax.experimental import pallas as pl
from jax.experimental.pallas import tpu as pltpu

F32 = jnp.float32
BF16 = jnp.bfloat16

D_MODEL = 2048
N_MEM = 256
A_HEADS = 8
A_DQK = 64
A_DV = 128
B_HEADS = 8
B_DK = 128
C_HEADS = 4
C_DH = 256
BRANCH_WIDTH = 1024
N_BRANCH = 3
EPS = 1e-6

COL_QA, COL_KA, COL_VA, COL_GA = 0, 8, 16, 24
COL_FB, COL_IB, COL_QB, COL_GB = 32, 40, 48, 56
COL_QC_1024, COL_GC_1024 = 8, 9
COL_GL_2048 = 5

VMEM_LIMIT_BYTES = 56 * 1024 * 1024

TILES = dict(
    proj_rows=1024,
    proj_cols_f32=1024,
    proj_cols_bf16=2048,
    attn_q=512,
    attn_kv=256,
    hgrn_rows=256,
    xattn_rows=512,
    merge_rows=256,
)

BF16_SUBLANES = 16
CHUNK = 128
MAX_CHUNK_DECAY = 160.0
SUB = 16


def _nt_dot(a, b):
    return lax.dot_general(a, b, (((1,), (1,)), ((), ())), preferred_element_type=F32)


def _tn_dot(a, b):
    return lax.dot_general(a, b, (((0,), (0,)), ((), ())), preferred_element_type=F32)


def _norm_proj_kernel(x_ref, g_ref, w_ref, *refs, emit_bf16_w, aliased_out):
    refs = refs[1:] if aliased_out else refs
    o_ref, h_ref = refs[0], refs[-1]

    @pl.when(pl.program_id(1) == 0)
    def _():
        x = x_ref[...]
        ms = jnp.mean(x * x, axis=-1, keepdims=True)
        h_ref[...] = (x * lax.rsqrt(ms + EPS) * g_ref[...]).astype(BF16)

    w = w_ref[...].astype(BF16)
    if emit_bf16_w:
        refs[1][...] = w
    o_ref[...] = jnp.dot(h_ref[...], w, preferred_element_type=F32).astype(o_ref.dtype)


def _norm_proj_call(kernel_kwargs, grid, in_specs, out_specs, out_shape, tm, k, **call_kwargs):
    return pl.pallas_call(
        functools.partial(_norm_proj_kernel, **kernel_kwargs),
        grid=grid, in_specs=in_specs, out_specs=out_specs, out_shape=out_shape,
        scratch_shapes=[pltpu.VMEM((tm, k), BF16)],
        compiler_params=pltpu.CompilerParams(
            dimension_semantics=("arbitrary", "arbitrary"),
            vmem_limit_bytes=VMEM_LIMIT_BYTES),
        name="norm_proj", **call_kwargs)


def _norm_proj(x, g, w, *, layer, tm, tn, tn_rest=None):
    m, k = x.shape
    n = w.shape[2]
    g = g.reshape(1, k)
    tn_rest = tn_rest or tn
    x_spec = lambda off: pl.BlockSpec((tm, k), lambda i, j: (i + off, 0))
    g_spec = pl.BlockSpec((1, k), lambda i, j: (0, 0))
    o_spec = lambda off, tn=tn: pl.BlockSpec((tm, tn), lambda i, j: (i + off, j))
    out = jax.ShapeDtypeStruct((m, n), BF16)
    first_only = m == tm
    res = _norm_proj_call(
        dict(emit_bf16_w=not first_only, aliased_out=False), (1, n // tn),
        [x_spec(0), g_spec, pl.BlockSpec((None, k, tn), lambda i, j: (layer, 0, j))],
        o_spec(0) if first_only else [o_spec(0), pl.BlockSpec((k, tn), lambda i, j: (0, j))],
        out if first_only else [out, jax.ShapeDtypeStruct((k, n), BF16)], tm, k,
    )(x, g, w)
    if first_only:
        return res
    out_head, w_bf16 = res
    return _norm_proj_call(
        dict(emit_bf16_w=False, aliased_out=True), (m // tm - 1, n // tn_rest),
        [x_spec(1), g_spec, pl.BlockSpec((k, tn_rest), lambda i, j: (0, j)),
         pl.BlockSpec(memory_space=pl.ANY)],
        o_spec(1, tn_rest), out, tm, k, input_output_aliases={3: 0},
    )(x, g, w_bf16, out_head)


def _attn_schedule(nq, nd):
    chains = []
    for c in range(nq // 2):
        own = (c, nq - 1 - c)
        steps = [(i, i * nd + d, s) for s, i in enumerate(own) for d in range(nd)]
        steps += [(i, j, s) for s, i in enumerate(own) for j in range(i * nd)]
        chains.append(steps)
    assert len({len(ch) for ch in chains}) == 1
    return np.asarray(chains, np.int32).transpose(1, 0, 2)


def _diff_attn_kernel(tab_ref, q_ref, k_ref, v_ref, gate_ref, lam_ref, g_ref, o_ref,
                      vt_ref, qt0_ref, qt1_ref, s0_ref, s1_ref, m_ref, acc_ref,
                      *, tq, tk, seq, lam_init):
    nq, nd = seq // tq, tq // tk
    n_chains = nq // 2
    qt_refs = (qt0_ref, qt1_ref)

    for c in range(seq // tk):
        vt_ref[:A_DV, c * tk:(c + 1) * tk] = v_ref[c * tk:(c + 1) * tk, :].T
    vt_ref[A_DV:, :] = jnp.ones((vt_ref.shape[0] - A_DV, seq), BF16)

    qscale = math.log2(math.e) * A_DQK ** -0.5
    sub = lax.broadcasted_iota(jnp.int32, (2 * A_DQK, tq), 0)
    for c in range(nq):
        qt = (q_ref[c * tq:(c + 1) * tq, :].astype(F32) * qscale).astype(BF16).T
        zero = jnp.zeros_like(qt)
        qt0_ref[:, c * tq:(c + 1) * tq] = jnp.where(sub < A_DQK, qt, zero)
        qt1_ref[:, c * tq:(c + 1) * tq] = jnp.where(sub >= A_DQK, qt, zero)

    m_ref[...] = jnp.full(m_ref.shape, -jnp.inf, F32)
    acc_ref[...] = jnp.zeros(acc_ref.shape, F32)

    s_refs = (s0_ref, s1_ref)

    assert nd == 2
    right = slice(tk, tq)

    def scores(t, par, diag=None):
        if diag is not None:
            causal = (lax.broadcasted_iota(jnp.int32, (tk, tk), 0)
                      <= lax.broadcasted_iota(jnp.int32, (tk, tk), 1))
        for c in range(n_chains):
            q0 = pl.multiple_of(tab_ref[t, c, 0] * tq, tq)
            k0 = pl.multiple_of(tab_ref[t, c, 1] * tk, tk)
            kb = k_ref[pl.ds(k0, tk), :]
            for mp in range(2):
                if diag is None:
                    s_refs[par][c, mp] = jnp.dot(kb, qt_refs[mp][:, pl.ds(q0, tq)],
                                                 preferred_element_type=F32)
                    continue
                ql = qt_refs[mp][:, pl.ds(q0, tk)]
                qr = qt_refs[mp][:, pl.ds(pl.multiple_of(q0 + tk, tk), tk)]
                if diag == 0:
                    s_refs[par][c, mp, :, :tk] = jnp.where(
                        causal, jnp.dot(kb, ql, preferred_element_type=F32), -jnp.inf)
                    s_refs[par][c, mp, :, right] = jnp.dot(kb, qr, preferred_element_type=F32)
                else:
                    s_refs[par][c, mp, :, right] = jnp.where(
                        causal, jnp.dot(kb, qr, preferred_element_type=F32), -jnp.inf)

    def update(t, par, cols=slice(None)):
        for c in range(n_chains):
            own = tab_ref[t, c, 2]
            k0 = pl.multiple_of(tab_ref[t, c, 1] * tk, tk)
            vtb = vt_ref[:, pl.ds(k0, tk)]
            for mp in range(2):
                s = s_refs[par][c, mp, :, cols]
                m_old = m_ref[c, own, mp, :, cols]
                m_new = jnp.maximum(m_old, jnp.max(s, axis=0, keepdims=True))
                p = jnp.exp2(s - m_new).astype(BF16)
                pv = jnp.dot(vtb, p, preferred_element_type=F32)
                acc_ref[c, own, mp, :, cols] = (
                    jnp.exp2(m_old - m_new) * acc_ref[c, own, mp, :, cols] + pv)
                m_ref[c, own, mp, :, cols] = m_new

    n_diag = 2 * nd
    n_trips = tab_ref.shape[0]
    assert (n_trips - n_diag) % 2 == 0
    scores(0, 0, diag=0)
    for t in range(n_diag):
        nxt = t + 1
        scores(nxt, nxt % 2, diag=(nxt % nd if nxt < n_diag else None))
        update(t, t % 2, cols=(right if t % nd == 1 else slice(None)))

    unroll = 4
    assert (n_trips - n_diag - 2) % unroll == 0

    def trips(u, carry):
        for v in range(unroll):
            t = n_diag + unroll * u + v
            par = (n_diag + v) % 2
            scores(t + 1, 1 - par)
            update(t, par)
        return carry

    lax.fori_loop(0, (n_trips - n_diag - 2) // unroll, trips, 0)
    scores(n_trips - 1, (n_trips - 1) % 2)
    update(n_trips - 2, (n_trips - 2) % 2)
    update(n_trips - 1, (n_trips - 1) % 2)

    lp = lam_ref[...]
    lam = (jnp.exp(jnp.sum(lp[0:1] * lp[1:2], axis=-1, keepdims=True))
           - jnp.exp(jnp.sum(lp[2:3] * lp[3:4], axis=-1, keepdims=True)) + lam_init)
    for c in range(n_chains):
        for own, qb in enumerate((c, nq - 1 - c)):
            inv_l1 = 1.0 / acc_ref[c, own, 0, A_DV:A_DV + 1]
            inv_l2 = lam / acc_ref[c, own, 1, A_DV:A_DV + 1]
            ot = acc_ref[c, own, 0, :A_DV] * inv_l1 - acc_ref[c, own, 1, :A_DV] * inv_l2
            ms = jnp.mean(ot * ot, axis=0, keepdims=True)
            ot = ot * (lax.rsqrt(ms + EPS) * (1.0 - lam_init)) * g_ref[...]
            rows = slice(qb * tq, (qb + 1) * tq)
            o_ref[rows, :] = (ot.T * jax.nn.silu(gate_ref[rows, :].astype(F32))
                              ).astype(o_ref.dtype)


def _diff_attn(p, lam_p, subln_g, *, batch, seq, lam_init, tq, tk):
    t = batch * seq
    nq, nd = seq // tq, tq // tk
    assert nq % 2 == 0
    table = _attn_schedule(nq, nd)
    kernel = functools.partial(_diff_attn_kernel, tq=tq, tk=tk, seq=seq, lam_init=lam_init)
    acc_rows = A_DV + BF16_SUBLANES
    return pl.pallas_call(
        kernel,
        grid=(batch, A_HEADS),
        in_specs=[
            pl.BlockSpec(memory_space=pltpu.SMEM),
            pl.BlockSpec((seq, 128), lambda b, h: (b, COL_QA + h)),
            pl.BlockSpec((seq, 128), lambda b, h: (b, COL_KA + h)),
            pl.BlockSpec((seq, 128), lambda b, h: (b, COL_VA + h)),
            pl.BlockSpec((seq, 128), lambda b, h: (b, COL_GA + h)),
            pl.BlockSpec((4, A_DQK), lambda b, h: (0, 0)),
            pl.BlockSpec((A_DV, 1), lambda b, h: (0, 0)),
        ],
        out_specs=pl.BlockSpec((seq, 128), lambda b, h: (b, h)),
        out_shape=jax.ShapeDtypeStruct((t, BRANCH_WIDTH), BF16),
        scratch_shapes=[
            pltpu.VMEM((acc_rows, seq), BF16),
            pltpu.VMEM((2 * A_DQK, seq), BF16),
            pltpu.VMEM((2 * A_DQK, seq), BF16),
            pltpu.VMEM((nq // 2, 2, tk, tq), F32),
            pltpu.VMEM((nq // 2, 2, tk, tq), F32),
            pltpu.VMEM((nq // 2, 2, 2, 1, tq), F32),
            pltpu.VMEM((nq // 2, 2, 2, acc_rows, tq), F32),
        ],
        compiler_params=pltpu.CompilerParams(
            dimension_semantics=("arbitrary", "arbitrary"),
            vmem_limit_bytes=VMEM_LIMIT_BYTES),
        name="diff_attn",
    )(jnp.asarray(table), p, p, p, p, lam_p, subln_g.reshape(A_DV, 1))


def _block_cumsum(x, rows, blk):
    r = lax.broadcasted_iota(jnp.int32, (rows, rows), 0)
    c = lax.broadcasted_iota(jnp.int32, (rows, rows), 1)
    tri = jnp.where((r // blk == c // blk) & (c <= r), 1.0, 0.0).astype(BF16)
    hi = x.astype(BF16)
    rem = x - hi.astype(F32)
    mid = rem.astype(BF16)
    lo = (rem - mid.astype(F32)).astype(BF16)
    return (jnp.dot(tri, hi, preferred_element_type=F32)
            + jnp.dot(tri, mid, preferred_element_type=F32)
            + jnp.dot(tri, lo, preferred_element_type=F32))


def _hgrn_kernel(f_ref, i_ref, q_ref, gate_ref, lbraw_ref, ng_ref, o_ref,
                 st_ref, b_s, q_s, lf_s, k_s, o_s, *, layer, rows):
    @pl.when(pl.program_id(1) == 0)
    def _():
        st_ref[...] = jnp.zeros(st_ref.shape, F32)

    raw = lbraw_ref[...]
    e = jnp.exp(raw - jnp.max(raw, axis=0, keepdims=True))
    w = e / jnp.sum(e, axis=0, keepdims=True)
    lb = jnp.zeros((1, raw.shape[1]), F32)
    for j in range(1, layer + 1):
        lb = lb + w[j:j + 1]

    z = f_ref[...].astype(F32)
    e = jnp.exp(-jnp.abs(z))
    inv = 1.0 / (1.0 + e)
    log_sig = jnp.minimum(z, 0.0) + jnp.log(inv)
    k_s[...] = (1.0 - lb) * (jnp.where(z >= 0.0, e, 1.0) * inv)
    a = jnp.log(lb)
    c = jnp.log1p(-lb) + log_sig
    logf = jnp.maximum(a, c) + jnp.log(1.0 + jnp.exp(-jnp.abs(a - c)))
    qf = q_ref[...].astype(F32)
    q_s[...] = qf / (1.0 + jnp.exp(-qf))
    lf_s[...] = logf

    b_s[...] = _block_cumsum(logf, rows, CHUNK)
    totals = jnp.concatenate(
        [b_s[c * CHUNK + CHUNK - 1:(c + 1) * CHUNK, :] for c in range(rows // CHUNK)], axis=0)
    fast = jnp.max(-totals) <= MAX_CHUNK_DECAY

    @pl.when(fast)
    def _():
        t_i = lax.broadcasted_iota(jnp.int32, (CHUNK, CHUNK), 0)
        s_i = lax.broadcasted_iota(jnp.int32, (CHUNK, CHUNK), 1)
        causal = s_i <= t_i
        for c in range(rows // CHUNK):
            rs = slice(c * CHUNK, (c + 1) * CHUNK)
            weights, inter = [], []
            for h in range(B_HEADS):
                cs = slice(h * B_DK, (h + 1) * B_DK)
                bn, qn, kn, vn = b_s[rs, cs], q_s[rs, cs], k_s[rs, cs], i_ref[rs, cs]
                bl = bn[CHUNK - 1:CHUNK]
                half = 0.5 * bl
                qp = (qn * jnp.exp(bn - half)).astype(BF16)
                kp = (kn * jnp.exp(half - bn)).astype(BF16)
                weights.append(_nt_dot(qp, kp))
                st = st_ref[h]
                qe = (qn * jnp.exp(bn)).astype(BF16)
                inter.append(_nt_dot(qe, st.astype(BF16)))
                kd = (kn * jnp.exp(bl - bn)).astype(BF16)
                st_ref[h] = st * jnp.exp(bl) + _tn_dot(vn, kd)
            for h in range(B_HEADS):
                cs = slice(h * B_DK, (h + 1) * B_DK)
                a = jnp.where(causal, weights[h], 0.0).astype(BF16)
                o_s[rs, cs] = inter[h] + jnp.dot(a, i_ref[rs, cs], preferred_element_type=F32)

    @pl.when(jnp.logical_not(fast))
    def _():
        b_s[...] = _block_cumsum(lf_s[...], rows, SUB)
        t_idx = lax.broadcasted_iota(jnp.int32, (SUB, 1), 0)

        def body(n, carry):
            r0 = pl.multiple_of(n * SUB, SUB)
            for h in range(B_HEADS):
                cs = slice(h * B_DK, (h + 1) * B_DK)
                bn = b_s[pl.ds(r0, SUB), cs]
                qn = q_s[pl.ds(r0, SUB), cs]
                kn = k_s[pl.ds(r0, SUB), cs]
                vn = i_ref[pl.ds(r0, SUB), cs]
                vf = vn.astype(F32)
                st = st_ref[h]
                acc = _nt_dot((qn * jnp.exp(bn)).astype(BF16), st.astype(BF16))
                for s in range(SUB):
                    arg = jnp.where(t_idx >= s, bn - bn[s:s + 1], -jnp.inf)
                    wts = qn * (kn[s:s + 1] * jnp.exp(arg))
                    acc = acc + jnp.sum(wts, axis=-1, keepdims=True) * vf[s:s + 1]
                o_s[pl.ds(r0, SUB), cs] = acc
                bl = bn[SUB - 1:SUB]
                kd = kn * jnp.exp(bl - bn)
                st_ref[h] = st * jnp.exp(bl) + _tn_dot(vn, kd.astype(BF16))
            return carry

        lax.fori_loop(0, rows // SUB, body, 0)

    for h in range(B_HEADS):
        cs = slice(h * B_DK, (h + 1) * B_DK)
        o = o_s[:, cs]
        ms = jnp.mean(o * o, axis=-1, keepdims=True)
        y = o * lax.rsqrt(ms + EPS) * ng_ref[...]
        o_ref[:, cs] = (y * jax.nn.silu(gate_ref[:, cs].astype(F32))).astype(o_ref.dtype)


def _hgrn(p, lb_raw, norm_g, *, batch, seq, layer, rows):
    t = batch * seq
    nr = seq // rows
    kernel = functools.partial(_hgrn_kernel, layer=layer, rows=rows)
    blk = lambda col: pl.BlockSpec((rows, 1024), lambda b, i: (b * nr + i, col // 8))
    depth = lb_raw.shape[0]
    return pl.pallas_call(
        kernel,
        grid=(batch, nr),
        in_specs=[
            blk(COL_FB), blk(COL_IB), blk(COL_QB), blk(COL_GB),
            pl.BlockSpec((depth, 1024), lambda b, i: (0, 0)),
            pl.BlockSpec((1, B_DK), lambda b, i: (0, 0)),
        ],
        out_specs=pl.BlockSpec((rows, 1024), lambda b, i: (b * nr + i, 0)),
        out_shape=jax.ShapeDtypeStruct((t, BRANCH_WIDTH), BF16),
        scratch_shapes=[
            pltpu.VMEM((B_HEADS, B_DK, B_DK), F32),
            pltpu.VMEM((rows, 1024), F32),
            pltpu.VMEM((rows, 1024), F32),
            pltpu.VMEM((rows, 1024), F32),
            pltpu.VMEM((rows, 1024), F32),
            pltpu.VMEM((rows, 1024), F32),
        ],
        compiler_params=pltpu.CompilerParams(
            dimension_semantics=("arbitrary", "arbitrary"),
            vmem_limit_bytes=VMEM_LIMIT_BYTES),
        name="hgrn2",
    )(p, p, p, p, lb_raw, norm_g.reshape(1, B_DK))


def _xattn_kernel(q_ref, gate_ref, k_ref, v_ref, o_ref):
    heads = [slice(h * C_DH, (h + 1) * C_DH) for h in range(C_HEADS)]
    scores = [_nt_dot(q_ref[:, cs], k_ref[:, cs]) for cs in heads]
    c = (C_DH ** -0.5) * math.log2(math.e)
    for cs, s in zip(heads, scores):
        m = jnp.max(s, axis=-1, keepdims=True)
        pr = jnp.exp2((s - m) * c)
        inv_l = 1.0 / jnp.sum(pr, axis=-1, keepdims=True)
        o = jnp.dot(pr.astype(BF16), v_ref[:, cs], preferred_element_type=F32) * inv_l
        o_ref[:, cs] = (o * jax.nn.silu(gate_ref[:, cs].astype(F32))).astype(o_ref.dtype)


def _xattn(p, kv, *, batch, seq, tq):
    t = batch * seq
    nq = seq // tq
    return pl.pallas_call(
        _xattn_kernel,
        grid=(batch, nq),
        in_specs=[
            pl.BlockSpec((tq, 1024), lambda b, i: (b * nq + i, COL_QC_1024)),
            pl.BlockSpec((tq, 1024), lambda b, i: (b * nq + i, COL_GC_1024)),
            pl.BlockSpec((N_MEM, 1024), lambda b, i: (b, 0)),
            pl.BlockSpec((N_MEM, 1024), lambda b, i: (b, 1)),
        ],
        out_specs=pl.BlockSpec((tq, 1024), lambda b, i: (b * nq + i, 0)),
        out_shape=jax.ShapeDtypeStruct((t, BRANCH_WIDTH), BF16),
        compiler_params=pltpu.CompilerParams(
            dimension_semantics=("arbitrary", "arbitrary"),
            vmem_limit_bytes=VMEM_LIMIT_BYTES),
        name="mem_xattn",
    )(p, p, kv, kv)


def _merge_kernel(oa_ref, ob_ref, oc_ref, g0_ref, g1_ref, g2_ref, x_ref, wb_ref, wo_ref,
                  fg_ref, o_ref, *, final):
    half = D_MODEL // 2
    ys = []
    for cols in (slice(0, half), slice(half, D_MODEL)):
        y = None
        for o_r, g_r, j in ((oa_ref, g0_ref, 0), (ob_ref, g1_ref, 1), (oc_ref, g2_ref, 2)):
            term = jax.nn.sigmoid(g_r[:, cols].astype(F32)) * jnp.dot(
                o_r[...], wb_ref[j, :, cols], preferred_element_type=F32)
            y = term if y is None else y + term
        ys.append(y.astype(BF16))
    xn = x_ref[...] + jnp.dot(ys[0], wo_ref[:half, :], preferred_element_type=F32)
    xn = xn + jnp.dot(ys[1], wo_ref[half:, :], preferred_element_type=F32)
    if final:
        ms = jnp.mean(xn * xn, axis=-1, keepdims=True)
        xn = xn * lax.rsqrt(ms + EPS) * fg_ref[...]
    o_ref[...] = xn


def _merge(oa, ob, oc, p, x, wb, wo, final_g, *, layer, tm, final):
    t = x.shape[0]
    row = lambda i: (i, 0)
    const2 = lambda i: (0, 0)
    kernel = functools.partial(_merge_kernel, final=final)
    return pl.pallas_call(
        kernel,
        grid=(t // tm,),
        in_specs=[
            pl.BlockSpec((tm, BRANCH_WIDTH), row),
            pl.BlockSpec((tm, BRANCH_WIDTH), row),
            pl.BlockSpec((tm, BRANCH_WIDTH), row),
            pl.BlockSpec((tm, D_MODEL), lambda i: (i, COL_GL_2048)),
            pl.BlockSpec((tm, D_MODEL), lambda i: (i, COL_GL_2048 + 1)),
            pl.BlockSpec((tm, D_MODEL), lambda i: (i, COL_GL_2048 + 2)),
            pl.BlockSpec((tm, D_MODEL), row),
            pl.BlockSpec((None, N_BRANCH, BRANCH_WIDTH, D_MODEL), lambda i: (layer, 0, 0, 0),
                         pipeline_mode=pl.Buffered(1)),
            pl.BlockSpec((None, D_MODEL, D_MODEL), lambda i: (layer, 0, 0),
                         pipeline_mode=pl.Buffered(1)),
            pl.BlockSpec((1, D_MODEL), const2),
        ],
        out_specs=pl.BlockSpec((tm, D_MODEL), row),
        out_shape=jax.ShapeDtypeStruct((t, D_MODEL), F32),
        compiler_params=pltpu.CompilerParams(
            dimension_semantics=("arbitrary",),
            vmem_limit_bytes=VMEM_LIMIT_BYTES),
        name="merge_out",
    )(oa, ob, oc, p, p, p, x, wb, wo, final_g.reshape(1, D_MODEL))


def kernel(x, mem, norm_g, w_in, diff_lambda, diff_subln_g, hgrn_lb_raw, hgrn_norm_g,
           mem_norm_g, w_kv_mem, w_branch, w_out, final_norm_g):
    batch, seq, d = x.shape
    depth = w_in.shape[0]
    xt = x.reshape(batch * seq, d)
    memt = mem.reshape(batch * N_MEM, d)
    w_br_b = w_branch.astype(BF16)
    w_out_b = w_out.astype(BF16)
    lb_raw = hgrn_lb_raw.astype(F32)
    t = TILES
    for l in range(depth):
        lam_init = 0.8 - 0.6 * math.exp(-0.3 * l)
        p = _norm_proj(xt, norm_g[l], w_in, layer=l, tm=t["proj_rows"],
                       tn=t["proj_cols_f32"], tn_rest=t["proj_cols_bf16"])
        kv = _norm_proj(memt, mem_norm_g[l], w_kv_mem, layer=l, tm=t["proj_rows"],
                        tn=t["proj_cols_f32"])
        oa = _diff_attn(p, diff_lambda[l], diff_subln_g[l], batch=batch, seq=seq,
                        lam_init=lam_init, tq=t["attn_q"], tk=t["attn_kv"])
        ob = _hgrn(p, lb_raw, hgrn_norm_g[l], batch=batch, seq=seq, layer=l,
                   rows=t["hgrn_rows"])
        oc = _xattn(p, kv, batch=batch, seq=seq, tq=t["xattn_rows"])
        xt = _merge(oa, ob, oc, p, xt, w_br_b, w_out_b, final_norm_g,
                    layer=l, tm=t["merge_rows"], final=(l == depth - 1))
    return xt.reshape(batch, seq, d)
```

```python
import functools
import math

import jax
import jax.numpy as jnp
import numpy as np
from jax import lax
from jax.experimental import pallas as pl
from jax.experimental.pallas import tpu as pltpu

F32 = jnp.float32
BF16 = jnp.bfloat16

D_MODEL = 2048
N_MEM = 256
A_HEADS = 8
A_DQK = 64
A_DV = 128
B_HEADS = 8
B_DK = 128
C_HEADS = 4
C_DH = 256
BRANCH_WIDTH = 1024
N_BRANCH = 3
EPS = 1e-6

COL_QA, COL_KA, COL_VA, COL_GA = 0, 8, 16, 24
COL_FB, COL_IB, COL_QB, COL_GB = 32, 40, 48, 56
COL_QC_1024, COL_GC_1024 = 8, 9
COL_GL_2048 = 5

VMEM_LIMIT_BYTES = 56 * 1024 * 1024

TILES = dict(
    proj_rows=1024,
    proj_cols_f32=1024,
    proj_cols_bf16=2048,
    attn_q=512,
    attn_kv=256,
    hgrn_rows=256,
    xattn_rows=512,
    merge_rows=256,
)

BF16_SUBLANES = 16
CHUNK = 128
MAX_CHUNK_DECAY = 160.0
SUB = 16


def _nt_dot(a, b):
    return lax.dot_general(a, b, (((1,), (1,)), ((), ())), preferred_element_type=F32)


def _tn_dot(a, b):
    return lax.dot_general(a, b, (((0,), (0,)), ((), ())), preferred_element_type=F32)


def _norm_proj_kernel(x_ref, g_ref, w_ref, *refs, emit_bf16_w, aliased_out):
    refs = refs[1:] if aliased_out else refs
    o_ref, h_ref = refs[0], refs[-1]

    @pl.when(pl.program_id(1) == 0)
    def _():
        x = x_ref[...]
        ms = jnp.mean(x * x, axis=-1, keepdims=True)
        h_ref[...] = (x * lax.rsqrt(ms + EPS) * g_ref[...]).astype(BF16)

    w = w_ref[...].astype(BF16)
    if emit_bf16_w:
        refs[1][...] = w
    o_ref[...] = jnp.dot(h_ref[...], w, preferred_element_type=F32).astype(o_ref.dtype)


def _norm_proj_call(kernel_kwargs, grid, in_specs, out_specs, out_shape, tm, k, **call_kwargs):
    return pl.pallas_call(
        functools.partial(_norm_proj_kernel, **kernel_kwargs),
        grid=grid, in_specs=in_specs, out_specs=out_specs, out_shape=out_shape,
        scratch_shapes=[pltpu.VMEM((tm, k), BF16)],
        compiler_params=pltpu.CompilerParams(
            dimension_semantics=("arbitrary", "arbitrary"),
            vmem_limit_bytes=VMEM_LIMIT_BYTES),
        name="norm_proj", **call_kwargs)


def _norm_proj(x, g, w, *, layer, tm, tn, tn_rest=None):
    m, k = x.shape
    n = w.shape[2]
    g = g.reshape(1, k)
    tn_rest = tn_rest or tn
    x_spec = lambda off: pl.BlockSpec((tm, k), lambda i, j: (i + off, 0))
    g_spec = pl.BlockSpec((1, k), lambda i, j: (0, 0))
    o_spec = lambda off, tn=tn: pl.BlockSpec((tm, tn), lambda i, j: (i + off, j))
    out = jax.ShapeDtypeStruct((m, n), BF16)
    first_only = m == tm
    res = _norm_proj_call(
        dict(emit_bf16_w=not first_only, aliased_out=False), (1, n // tn),
        [x_spec(0), g_spec, pl.BlockSpec((None, k, tn), lambda i, j: (layer, 0, j))],
        o_spec(0) if first_only else [o_spec(0), pl.BlockSpec((k, tn), lambda i, j: (0, j))],
        out if first_only else [out, jax.ShapeDtypeStruct((k, n), BF16)], tm, k,
    )(x, g, w)
    if first_only:
        return res
    out_head, w_bf16 = res
    return _norm_proj_call(
        dict(emit_bf16_w=False, aliased_out=True), (m // tm - 1, n // tn_rest),
        [x_spec(1), g_spec, pl.BlockSpec((k, tn_rest), lambda i, j: (0, j)),
         pl.BlockSpec(memory_space=pl.ANY)],
        o_spec(1, tn_rest), out, tm, k, input_output_aliases={3: 0},
    )(x, g, w_bf16, out_head)


def _attn_schedule(nq, nd):
    chains = []
    for c in range(nq // 2):
        own = (c, nq - 1 - c)
        steps = [(i, i * nd + d, s) for s, i in enumerate(own) for d in range(nd)]
        steps += [(i, j, s) for s, i in enumerate(own) for j in range(i * nd)]
        chains.append(steps)
    assert len({len(ch) for ch in chains}) == 1
    return np.asarray(chains, np.int32).transpose(1, 0, 2)


def _diff_attn_kernel(tab_ref, q_ref, k_ref, v_ref, gate_ref, lam_ref, g_ref, o_ref,
                      vt_ref, qt0_ref, qt1_ref, s0_ref, s1_ref, m_ref, acc_ref,
                      *, tq, tk, seq, lam_init):
    nq, nd = seq // tq, tq // tk
    n_chains = nq // 2
    qt_refs = (qt0_ref, qt1_ref)

    for c in range(seq // tk):
        vt_ref[:A_DV, c * tk:(c + 1) * tk] = v_ref[c * tk:(c + 1) * tk, :].T
    vt_ref[A_DV:, :] = jnp.ones((vt_ref.shape[0] - A_DV, seq), BF16)

    qscale = math.log2(math.e) * A_DQK ** -0.5
    sub = lax.broadcasted_iota(jnp.int32, (2 * A_DQK, tq), 0)
    for c in range(nq):
        qt = (q_ref[c * tq:(c + 1) * tq, :].astype(F32) * qscale).astype(BF16).T
        zero = jnp.zeros_like(qt)
        qt0_ref[:, c * tq:(c + 1) * tq] = jnp.where(sub < A_DQK, qt, zero)
        qt1_ref[:, c * tq:(c + 1) * tq] = jnp.where(sub >= A_DQK, qt, zero)

    m_ref[...] = jnp.full(m_ref.shape, -jnp.inf, F32)
    acc_ref[...] = jnp.zeros(acc_ref.shape, F32)

    s_refs = (s0_ref, s1_ref)

    assert nd == 2
    right = slice(tk, tq)

    def scores(t, par, diag=None):
        if diag is not None:
            causal = (lax.broadcasted_iota(jnp.int32, (tk, tk), 0)
                      <= lax.broadcasted_iota(jnp.int32, (tk, tk), 1))
        for c in range(n_chains):
            q0 = pl.multiple_of(tab_ref[t, c, 0] * tq, tq)
            k0 = pl.multiple_of(tab_ref[t, c, 1] * tk, tk)
            kb = k_ref[pl.ds(k0, tk), :]
            for mp in range(2):
                if diag is None:
                    s_refs[par][c, mp] = jnp.dot(kb, qt_refs[mp][:, pl.ds(q0, tq)],
                                                 preferred_element_type=F32)
                    continue
                ql = qt_refs[mp][:, pl.ds(q0, tk)]
                qr = qt_refs[mp][:, pl.ds(pl.multiple_of(q0 + tk, tk), tk)]
                if diag == 0:
                    s_refs[par][c, mp, :, :tk] = jnp.where(
                        causal, jnp.dot(kb, ql, preferred_element_type=F32), -jnp.inf)
                    s_refs[par][c, mp, :, right] = jnp.dot(kb, qr, preferred_element_type=F32)
                else:
                    s_refs[par][c, mp, :, right] = jnp.where(
                        causal, jnp.dot(kb, qr, preferred_element_type=F32), -jnp.inf)

    def update(t, par, cols=slice(None)):
        for c in range(n_chains):
            own = tab_ref[t, c, 2]
            k0 = pl.multiple_of(tab_ref[t, c, 1] * tk, tk)
            vtb = vt_ref[:, pl.ds(k0, tk)]
            for mp in range(2):
                s = s_refs[par][c, mp, :, cols]
                m_old = m_ref[c, own, mp, :, cols]
                m_new = jnp.maximum(m_old, jnp.max(s, axis=0, keepdims=True))
                p = jnp.exp2(s - m_new).astype(BF16)
                pv = jnp.dot(vtb, p, preferred_element_type=F32)
                acc_ref[c, own, mp, :, cols] = (
                    jnp.exp2(m_old - m_new) * acc_ref[c, own, mp, :, cols] + pv)
                m_ref[c, own, mp, :, cols] = m_new

    n_diag = 2 * nd
    n_trips = tab_ref.shape[0]
    assert (n_trips - n_diag) % 2 == 0
    scores(0, 0, diag=0)
    for t in range(n_diag):
        nxt = t + 1
        scores(nxt, nxt % 2, diag=(nxt % nd if nxt < n_diag else None))
        update(t, t % 2, cols=(right if t % nd == 1 else slice(None)))

    unroll = 4
    assert (n_trips - n_diag - 2) % unroll == 0

    def trips(u, carry):
        for v in range(unroll):
            t = n_diag + unroll * u + v
            par = (n_diag + v) % 2
            scores(t + 1, 1 - par)
            update(t, par)
        return carry

    lax.fori_loop(0, (n_trips - n_diag - 2) // unroll, trips, 0)
    scores(n_trips - 1, (n_trips - 1) % 2)
    update(n_trips - 2, (n_trips - 2) % 2)
    update(n_trips - 1, (n_trips - 1) % 2)

    lp = lam_ref[...]
    lam = (jnp.exp(jnp.sum(lp[0:1] * lp[1:2], axis=-1, keepdims=True))
           - jnp.exp(jnp.sum(lp[2:3] * lp[3:4], axis=-1, keepdims=True)) + lam_init)
    for c in range(n_chains):
        for own, qb in enumerate((c, nq - 1 - c)):
            inv_l1 = 1.0 / acc_ref[c, own, 0, A_DV:A_DV + 1]
            inv_l2 = lam / acc_ref[c, own, 1, A_DV:A_DV + 1]
            ot = acc_ref[c, own, 0, :A_DV] * inv_l1 - acc_ref[c, own, 1, :A_DV] * inv_l2
            ms = jnp.mean(ot * ot, axis=0, keepdims=True)
            ot = ot * (lax.rsqrt(ms + EPS) * (1.0 - lam_init)) * g_ref[...]
            rows = slice(qb * tq, (qb + 1) * tq)
            o_ref[rows, :] = (ot.T * jax.nn.silu(gate_ref[rows, :].astype(F32))
                              ).astype(o_ref.dtype)


def _diff_attn(p, lam_p, subln_g, *, batch, seq, lam_init, tq, tk):
    t = batch * seq
    nq, nd = seq // tq, tq // tk
    assert nq % 2 == 0
    table = _attn_schedule(nq, nd)
    kernel = functools.partial(_diff_attn_kernel, tq=tq, tk=tk, seq=seq, lam_init=lam_init)
    acc_rows = A_DV + BF16_SUBLANES
    return pl.pallas_call(
        kernel,
        grid=(batch, A_HEADS),
        in_specs=[
            pl.BlockSpec(memory_space=pltpu.SMEM),
            pl.BlockSpec((seq, 128), lambda b, h: (b, COL_QA + h)),
            pl.BlockSpec((seq, 128), lambda b, h: (b, COL_KA + h)),
            pl.BlockSpec((seq, 128), lambda b, h: (b, COL_VA + h)),
            pl.BlockSpec((seq, 128), lambda b, h: (b, COL_GA + h)),
            pl.BlockSpec((4, A_DQK), lambda b, h: (0, 0)),
            pl.BlockSpec((A_DV, 1), lambda b, h: (0, 0)),
        ],
        out_specs=pl.BlockSpec((seq, 128), lambda b, h: (b, h)),
        out_shape=jax.ShapeDtypeStruct((t, BRANCH_WIDTH), BF16),
        scratch_shapes=[
            pltpu.VMEM((acc_rows, seq), BF16),
            pltpu.VMEM((2 * A_DQK, seq), BF16),
            pltpu.VMEM((2 * A_DQK, seq), BF16),
            pltpu.VMEM((nq // 2, 2, tk, tq), F32),
            pltpu.VMEM((nq // 2, 2, tk, tq), F32),
            pltpu.VMEM((nq // 2, 2, 2, 1, tq), F32),
            pltpu.VMEM((nq // 2, 2, 2, acc_rows, tq), F32),
        ],
        compiler_params=pltpu.CompilerParams(
            dimension_semantics=("arbitrary", "arbitrary"),
            vmem_limit_bytes=VMEM_LIMIT_BYTES),
        name="diff_attn",
    )(jnp.asarray(table), p, p, p, p, lam_p, subln_g.reshape(A_DV, 1))


def _block_cumsum(x, rows, blk):
    r = lax.broadcasted_iota(jnp.int32, (rows, rows), 0)
    c = lax.broadcasted_iota(jnp.int32, (rows, rows), 1)
    tri = jnp.where((r // blk == c // blk) & (c <= r), 1.0, 0.0).astype(BF16)
    hi = x.astype(BF16)
    rem = x - hi.astype(F32)
    mid = rem.astype(BF16)
    lo = (rem - mid.astype(F32)).astype(BF16)
    return (jnp.dot(tri, hi, preferred_element_type=F32)
            + jnp.dot(tri, mid, preferred_element_type=F32)
            + jnp.dot(tri, lo, preferred_element_type=F32))


def _hgrn_kernel(f_ref, i_ref, q_ref, gate_ref, lbraw_ref, ng_ref, o_ref,
                 st_ref, b_s, q_s, lf_s, k_s, o_s, *, layer, rows):
    @pl.when(pl.program_id(1) == 0)
    def _():
        st_ref[...] = jnp.zeros(st_ref.shape, F32)

    raw = lbraw_ref[...]
    e = jnp.exp(raw - jnp.max(raw, axis=0, keepdims=True))
    w = e / jnp.sum(e, axis=0, keepdims=True)
    lb = jnp.zeros((1, raw.shape[1]), F32)
    for j in range(1, layer + 1):
        lb = lb + w[j:j + 1]

    z = f_ref[...].astype(F32)
    e = jnp.exp(-jnp.abs(z))
    inv = 1.0 / (1.0 + e)
    log_sig = jnp.minimum(z, 0.0) + jnp.log(inv)
    k_s[...] = (1.0 - lb) * (jnp.where(z >= 0.0, e, 1.0) * inv)
    a = jnp.log(lb)
    c = jnp.log1p(-lb) + log_sig
    logf = jnp.maximum(a, c) + jnp.log(1.0 + jnp.exp(-jnp.abs(a - c)))
    qf = q_ref[...].astype(F32)
    q_s[...] = qf / (1.0 + jnp.exp(-qf))
    lf_s[...] = logf

    b_s[...] = _block_cumsum(logf, rows, CHUNK)
    totals = jnp.concatenate(
        [b_s[c * CHUNK + CHUNK - 1:(c + 1) * CHUNK, :] for c in range(rows // CHUNK)], axis=0)
    fast = jnp.max(-totals) <= MAX_CHUNK_DECAY

    def finish(rs):
        for h in range(B_HEADS):
            cs = slice(h * B_DK, (h + 1) * B_DK)
            o = o_s[rs, cs]
            ms = jnp.mean(o * o, axis=-1, keepdims=True)
            y = o * lax.rsqrt(ms + EPS) * ng_ref[...]
            o_ref[rs, cs] = (y * jax.nn.silu(gate_ref[rs, cs].astype(F32))).astype(o_ref.dtype)

    @pl.when(fast)
    def _():
        t_i = lax.broadcasted_iota(jnp.int32, (CHUNK, CHUNK), 0)
        s_i = lax.broadcasted_iota(jnp.int32, (CHUNK, CHUNK), 1)
        causal = s_i <= t_i
        for c in range(rows // CHUNK):
            rs = slice(c * CHUNK, (c + 1) * CHUNK)
            weights, inter = [], []
            for h in range(B_HEADS):
                cs = slice(h * B_DK, (h + 1) * B_DK)
                bn, qn, kn, vn = b_s[rs, cs], q_s[rs, cs], k_s[rs, cs], i_ref[rs, cs]
                bl = bn[CHUNK - 1:CHUNK]
                half = 0.5 * bl
                qp = (qn * jnp.exp(bn - half)).astype(BF16)
                kp = (kn * jnp.exp(half - bn)).astype(BF16)
                weights.append(_nt_dot(qp, kp))
                st = st_ref[h]
                qe = (qn * jnp.exp(bn)).astype(BF16)
                inter.append(_nt_dot(qe, st.astype(BF16)))
                kd = (kn * jnp.exp(bl - bn)).astype(BF16)
                st_ref[h] = st * jnp.exp(bl) + _tn_dot(vn, kd)
            for h in range(B_HEADS):
                cs = slice(h * B_DK, (h + 1) * B_DK)
                a = jnp.where(causal, weights[h], 0.0).astype(BF16)
                o_s[rs, cs] = inter[h] + jnp.dot(a, i_ref[rs, cs], preferred_element_type=F32)
            finish(rs)

    @pl.when(jnp.logical_not(fast))
    def _():
        b_s[...] = _block_cumsum(lf_s[...], rows, SUB)
        t_idx = lax.broadcasted_iota(jnp.int32, (SUB, 1), 0)

        def body(n, carry):
            r0 = pl.multiple_of(n * SUB, SUB)
            for h in range(B_HEADS):
                cs = slice(h * B_DK, (h + 1) * B_DK)
                bn = b_s[pl.ds(r0, SUB), cs]
                qn = q_s[pl.ds(r0, SUB), cs]
                kn = k_s[pl.ds(r0, SUB), cs]
                vn = i_ref[pl.ds(r0, SUB), cs]
                vf = vn.astype(F32)
                st = st_ref[h]
                acc = _nt_dot((qn * jnp.exp(bn)).astype(BF16), st.astype(BF16))
                for s in range(SUB):
                    arg = jnp.where(t_idx >= s, bn - bn[s:s + 1], -jnp.inf)
                    wts = qn * (kn[s:s + 1] * jnp.exp(arg))
                    acc = acc + jnp.sum(wts, axis=-1, keepdims=True) * vf[s:s + 1]
                o_s[pl.ds(r0, SUB), cs] = acc
                bl = bn[SUB - 1:SUB]
                kd = kn * jnp.exp(bl - bn)
                st_ref[h] = st * jnp.exp(bl) + _tn_dot(vn, kd.astype(BF16))
            return carry

        lax.fori_loop(0, rows // SUB, body, 0)
        finish(slice(0, rows))


def _hgrn(p, lb_raw, norm_g, *, batch, seq, layer, rows):
    t = batch * seq
    nr = seq // rows
    kernel = functools.partial(_hgrn_kernel, layer=layer, rows=rows)
    blk = lambda col: pl.BlockSpec((rows, 1024), lambda b, i: (b * nr + i, col // 8))
    depth = lb_raw.shape[0]
    return pl.pallas_call(
        kernel,
        grid=(batch, nr),
        in_specs=[
            blk(COL_FB), blk(COL_IB), blk(COL_QB), blk(COL_GB),
            pl.BlockSpec((depth, 1024), lambda b, i: (0, 0)),
            pl.BlockSpec((1, B_DK), lambda b, i: (0, 0)),
        ],
        out_specs=pl.BlockSpec((rows, 1024), lambda b, i: (b * nr + i, 0)),
        out_shape=jax.ShapeDtypeStruct((t, BRANCH_WIDTH), BF16),
        scratch_shapes=[
            pltpu.VMEM((B_HEADS, B_DK, B_DK), F32),
            pltpu.VMEM((rows, 1024), F32),
            pltpu.VMEM((rows, 1024), F32),
            pltpu.VMEM((rows, 1024), F32),
            pltpu.VMEM((rows, 1024), F32),
            pltpu.VMEM((rows, 1024), F32),
        ],
        compiler_params=pltpu.CompilerParams(
            dimension_semantics=("arbitrary", "arbitrary"),
            vmem_limit_bytes=VMEM_LIMIT_BYTES),
        name="hgrn2",
    )(p, p, p, p, lb_raw, norm_g.reshape(1, B_DK))


def _xattn_kernel(q_ref, gate_ref, k_ref, v_ref, o_ref):
    heads = [slice(h * C_DH, (h + 1) * C_DH) for h in range(C_HEADS)]
    scores = [_nt_dot(q_ref[:, cs], k_ref[:, cs]) for cs in heads]
    c = (C_DH ** -0.5) * math.log2(math.e)
    for cs, s in zip(heads, scores):
        m = jnp.max(s, axis=-1, keepdims=True)
        pr = jnp.exp2((s - m) * c)
        inv_l = 1.0 / jnp.sum(pr, axis=-1, keepdims=True)
        o = jnp.dot(pr.astype(BF16), v_ref[:, cs], preferred_element_type=F32) * inv_l
        o_ref[:, cs] = (o * jax.nn.silu(gate_ref[:, cs].astype(F32))).astype(o_ref.dtype)


def _xattn(p, kv, *, batch, seq, tq):
    t = batch * seq
    nq = seq // tq
    return pl.pallas_call(
        _xattn_kernel,
        grid=(batch, nq),
        in_specs=[
            pl.BlockSpec((tq, 1024), lambda b, i: (b * nq + i, COL_QC_1024)),
            pl.BlockSpec((tq, 1024), lambda b, i: (b * nq + i, COL_GC_1024)),
            pl.BlockSpec((N_MEM, 1024), lambda b, i: (b, 0)),
            pl.BlockSpec((N_MEM, 1024), lambda b, i: (b, 1)),
        ],
        out_specs=pl.BlockSpec((tq, 1024), lambda b, i: (b * nq + i, 0)),
        out_shape=jax.ShapeDtypeStruct((t, BRANCH_WIDTH), BF16),
        compiler_params=pltpu.CompilerParams(
            dimension_semantics=("arbitrary", "arbitrary"),
            vmem_limit_bytes=VMEM_LIMIT_BYTES),
        name="mem_xattn",
    )(p, p, kv, kv)


def _merge_kernel(oa_ref, ob_ref, oc_ref, g0_ref, g1_ref, g2_ref, x_ref, wb_ref, wo_ref,
                  fg_ref, o_ref, *, final):
    half = D_MODEL // 2
    ys = []
    for cols in (slice(0, half), slice(half, D_MODEL)):
        y = None
        for o_r, g_r, j in ((oa_ref, g0_ref, 0), (ob_ref, g1_ref, 1), (oc_ref, g2_ref, 2)):
            term = jax.nn.sigmoid(g_r[:, cols].astype(F32)) * jnp.dot(
                o_r[...], wb_ref[j, :, cols], preferred_element_type=F32)
            y = term if y is None else y + term
        ys.append(y.astype(BF16))
    xn = x_ref[...] + jnp.dot(ys[0], wo_ref[:half, :], preferred_element_type=F32)
    xn = xn + jnp.dot(ys[1], wo_ref[half:, :], preferred_element_type=F32)
    if final:
        ms = jnp.mean(xn * xn, axis=-1, keepdims=True)
        xn = xn * lax.rsqrt(ms + EPS) * fg_ref[...]
    o_ref[...] = xn


def _merge(oa, ob, oc, p, x, wb, wo, final_g, *, layer, tm, final):
    t = x.shape[0]
    row = lambda i: (i, 0)
    const2 = lambda i: (0, 0)
    kernel = functools.partial(_merge_kernel, final=final)
    return pl.pallas_call(
        kernel,
        grid=(t // tm,),
        in_specs=[
            pl.BlockSpec((tm, BRANCH_WIDTH), row),
            pl.BlockSpec((tm, BRANCH_WIDTH), row),
            pl.BlockSpec((tm, BRANCH_WIDTH), row),
            pl.BlockSpec((tm, D_MODEL), lambda i: (i, COL_GL_2048)),
            pl.BlockSpec((tm, D_MODEL), lambda i: (i, COL_GL_2048 + 1)),
            pl.BlockSpec((tm, D_MODEL), lambda i: (i, COL_GL_2048 + 2)),
            pl.BlockSpec((tm, D_MODEL), row),
            pl.BlockSpec((None, N_BRANCH, BRANCH_WIDTH, D_MODEL), lambda i: (layer, 0, 0, 0),
                         pipeline_mode=pl.Buffered(1)),
            pl.BlockSpec((None, D_MODEL, D_MODEL), lambda i: (layer, 0, 0),
                         pipeline_mode=pl.Buffered(1)),
            pl.BlockSpec((1, D_MODEL), const2),
        ],
        out_specs=pl.BlockSpec((tm, D_MODEL), row),
        out_shape=jax.ShapeDtypeStruct((t, D_MODEL), F32),
        compiler_params=pltpu.CompilerParams(
            dimension_semantics=("arbitrary",),
            vmem_limit_bytes=VMEM_LIMIT_BYTES),
        name="merge_out",
    )(oa, ob, oc, p, p, p, x, wb, wo, final_g.reshape(1, D_MODEL))


def kernel(x, mem, norm_g, w_in, diff_lambda, diff_subln_g, hgrn_lb_raw, hgrn_norm_g,
           mem_norm_g, w_kv_mem, w_branch, w_out, final_norm_g):
    batch, seq, d = x.shape
    depth = w_in.shape[0]
    xt = x.reshape(batch * seq, d)
    memt = mem.reshape(batch * N_MEM, d)
    w_br_b = w_branch.astype(BF16)
    w_out_b = w_out.astype(BF16)
    lb_raw = hgrn_lb_raw.astype(F32)
    t = TILES
    for l in range(depth):
        lam_init = 0.8 - 0.6 * math.exp(-0.3 * l)
        p = _norm_proj(xt, norm_g[l], w_in, layer=l, tm=t["proj_rows"],
                       tn=t["proj_cols_f32"], tn_rest=t["proj_cols_bf16"])
        kv = _norm_proj(memt, mem_norm_g[l], w_kv_mem, layer=l, tm=t["proj_rows"],
                        tn=t["proj_cols_f32"])
        oa = _diff_attn(p, diff_lambda[l], diff_subln_g[l], batch=batch, seq=seq,
                        lam_init=lam_init, tq=t["attn_q"], tk=t["attn_kv"])
        ob = _hgrn(p, lb_raw, hgrn_norm_g[l], batch=batch, seq=seq, layer=l,
                   rows=t["hgrn_rows"])
        oc = _xattn(p, kv, batch=batch, seq=seq, tq=t["xattn_rows"])
        xt = _merge(oa, ob, oc, p, xt, w_br_b, w_out_b, final_norm_g,
                    layer=l, tm=t["merge_rows"], final=(l == depth - 1))
    return xt.reshape(batch, seq, d)
```

```python
import functools
import math

import jax
import jax.numpy as jnp
import numpy as np
from jax import lax
from jax.experimental import pallas as pl
from jax.experimental.pallas import tpu as pltpu

F32 = jnp.float32
BF16 = jnp.bfloat16

D_MODEL = 2048
N_MEM = 256
A_HEADS = 8
A_DQK = 64
A_DV = 128
B_HEADS = 8
B_DK = 128
C_HEADS = 4
C_DH = 256
BRANCH_WIDTH = 1024
N_BRANCH = 3
EPS = 1e-6

COL_QA, COL_KA, COL_VA, COL_GA = 0, 8, 16, 24
COL_FB, COL_IB, COL_QB, COL_GB = 32, 40, 48, 56
COL_QC_1024, COL_GC_1024 = 8, 9
COL_GL_2048 = 5

VMEM_LIMIT_BYTES = 56 * 1024 * 1024
MERGE_VMEM_LIMIT_BYTES = 62 * 1024 * 1024

TILES = dict(
    proj_rows=1024,
    proj_cols_f32=1024,
    proj_cols_bf16=2048,
    attn_q=512,
    attn_kv=256,
    hgrn_rows=256,
    xattn_rows=512,
    merge_rows=512,
)

BF16_SUBLANES = 16
CHUNK = 128
MAX_CHUNK_DECAY = 160.0
SUB = 16


def _nt_dot(a, b):
    return lax.dot_general(a, b, (((1,), (1,)), ((), ())), preferred_element_type=F32)


def _tn_dot(a, b):
    return lax.dot_general(a, b, (((0,), (0,)), ((), ())), preferred_element_type=F32)


def _norm_proj_kernel(x_ref, g_ref, w_ref, *refs, emit_bf16_w, aliased_out):
    refs = refs[1:] if aliased_out else refs
    o_ref, h_ref = refs[0], refs[-1]

    @pl.when(pl.program_id(1) == 0)
    def _():
        x = x_ref[...]
        ms = jnp.mean(x * x, axis=-1, keepdims=True)
        h_ref[...] = (x * lax.rsqrt(ms + EPS) * g_ref[...]).astype(BF16)

    w = w_ref[...].astype(BF16)
    if emit_bf16_w:
        refs[1][...] = w
    o_ref[...] = jnp.dot(h_ref[...], w, preferred_element_type=F32).astype(o_ref.dtype)


def _norm_proj_call(kernel_kwargs, grid, in_specs, out_specs, out_shape, tm, k, **call_kwargs):
    return pl.pallas_call(
        functools.partial(_norm_proj_kernel, **kernel_kwargs),
        grid=grid, in_specs=in_specs, out_specs=out_specs, out_shape=out_shape,
        scratch_shapes=[pltpu.VMEM((tm, k), BF16)],
        compiler_params=pltpu.CompilerParams(
            dimension_semantics=("arbitrary", "arbitrary"),
            vmem_limit_bytes=VMEM_LIMIT_BYTES),
        name="norm_proj", **call_kwargs)


def _norm_proj(x, g, w, *, layer, tm, tn, tn_rest=None):
    m, k = x.shape
    n = w.shape[2]
    g = g.reshape(1, k)
    tn_rest = tn_rest or tn
    x_spec = lambda off: pl.BlockSpec((tm, k), lambda i, j: (i + off, 0))
    g_spec = pl.BlockSpec((1, k), lambda i, j: (0, 0))
    o_spec = lambda off, tn=tn: pl.BlockSpec((tm, tn), lambda i, j: (i + off, j))
    out = jax.ShapeDtypeStruct((m, n), BF16)
    first_only = m == tm
    res = _norm_proj_call(
        dict(emit_bf16_w=not first_only, aliased_out=False), (1, n // tn),
        [x_spec(0), g_spec, pl.BlockSpec((None, k, tn), lambda i, j: (layer, 0, j))],
        o_spec(0) if first_only else [o_spec(0), pl.BlockSpec((k, tn), lambda i, j: (0, j))],
        out if first_only else [out, jax.ShapeDtypeStruct((k, n), BF16)], tm, k,
    )(x, g, w)
    if first_only:
        return res
    out_head, w_bf16 = res
    return _norm_proj_call(
        dict(emit_bf16_w=False, aliased_out=True), (m // tm - 1, n // tn_rest),
        [x_spec(1), g_spec, pl.BlockSpec((k, tn_rest), lambda i, j: (0, j)),
         pl.BlockSpec(memory_space=pl.ANY)],
        o_spec(1, tn_rest), out, tm, k, input_output_aliases={3: 0},
    )(x, g, w_bf16, out_head)


def _attn_schedule(nq, nd):
    chains = []
    for c in range(nq // 2):
        own = (c, nq - 1 - c)
        steps = [(i, i * nd + d, s) for s, i in enumerate(own) for d in range(nd)]
        steps += [(i, j, s) for s, i in enumerate(own) for j in range(i * nd)]
        chains.append(steps)
    assert len({len(ch) for ch in chains}) == 1
    return np.asarray(chains, np.int32).transpose(1, 0, 2)


def _diff_attn_kernel(tab_ref, q_ref, k_ref, v_ref, gate_ref, lam_ref, g_ref, o_ref,
                      vt_ref, qt0_ref, qt1_ref, s0_ref, s1_ref, m_ref, acc_ref,
                      *, tq, tk, seq, lam_init):
    nq, nd = seq // tq, tq // tk
    n_chains = nq // 2
    qt_refs = (qt0_ref, qt1_ref)

    for c in range(seq // tk):
        vt_ref[:A_DV, c * tk:(c + 1) * tk] = v_ref[c * tk:(c + 1) * tk, :].T
    vt_ref[A_DV:, :] = jnp.ones((vt_ref.shape[0] - A_DV, seq), BF16)

    qscale = math.log2(math.e) * A_DQK ** -0.5
    sub = lax.broadcasted_iota(jnp.int32, (2 * A_DQK, tq), 0)
    for c in range(nq):
        qt = (q_ref[c * tq:(c + 1) * tq, :].astype(F32) * qscale).astype(BF16).T
        zero = jnp.zeros_like(qt)
        qt0_ref[:, c * tq:(c + 1) * tq] = jnp.where(sub < A_DQK, qt, zero)
        qt1_ref[:, c * tq:(c + 1) * tq] = jnp.where(sub >= A_DQK, qt, zero)

    m_ref[...] = jnp.full(m_ref.shape, -jnp.inf, F32)
    acc_ref[...] = jnp.zeros(acc_ref.shape, F32)

    s_refs = (s0_ref, s1_ref)

    assert nd == 2
    right = slice(tk, tq)

    def scores(t, par, diag=None):
        if diag is not None:
            causal = (lax.broadcasted_iota(jnp.int32, (tk, tk), 0)
                      <= lax.broadcasted_iota(jnp.int32, (tk, tk), 1))
        for c in range(n_chains):
            q0 = pl.multiple_of(tab_ref[t, c, 0] * tq, tq)
            k0 = pl.multiple_of(tab_ref[t, c, 1] * tk, tk)
            kb = k_ref[pl.ds(k0, tk), :]
            for mp in range(2):
                if diag is None:
                    s_refs[par][c, mp] = jnp.dot(kb, qt_refs[mp][:, pl.ds(q0, tq)],
                                                 preferred_element_type=F32)
                    continue
                ql = qt_refs[mp][:, pl.ds(q0, tk)]
                qr = qt_refs[mp][:, pl.ds(pl.multiple_of(q0 + tk, tk), tk)]
                if diag == 0:
                    s_refs[par][c, mp, :, :tk] = jnp.where(
                        causal, jnp.dot(kb, ql, preferred_element_type=F32), -jnp.inf)
                    s_refs[par][c, mp, :, right] = jnp.dot(kb, qr, preferred_element_type=F32)
                else:
                    s_refs[par][c, mp, :, right] = jnp.where(
                        causal, jnp.dot(kb, qr, preferred_element_type=F32), -jnp.inf)

    def update(t, par, cols=slice(None)):
        for c in range(n_chains):
            own = tab_ref[t, c, 2]
            k0 = pl.multiple_of(tab_ref[t, c, 1] * tk, tk)
            vtb = vt_ref[:, pl.ds(k0, tk)]
            for mp in range(2):
                s = s_refs[par][c, mp, :, cols]
                m_old = m_ref[c, own, mp, :, cols]
                m_new = jnp.maximum(m_old, jnp.max(s, axis=0, keepdims=True))
                p = jnp.exp2(s - m_new).astype(BF16)
                pv = jnp.dot(vtb, p, preferred_element_type=F32)
                acc_ref[c, own, mp, :, cols] = (
                    jnp.exp2(m_old - m_new) * acc_ref[c, own, mp, :, cols] + pv)
                m_ref[c, own, mp, :, cols] = m_new

    n_diag = 2 * nd
    n_trips = tab_ref.shape[0]
    assert (n_trips - n_diag) % 2 == 0
    scores(0, 0, diag=0)
    for t in range(n_diag):
        nxt = t + 1
        scores(nxt, nxt % 2, diag=(nxt % nd if nxt < n_diag else None))
        update(t, t % 2, cols=(right if t % nd == 1 else slice(None)))

    unroll = 4
    assert (n_trips - n_diag - 2) % unroll == 0

    def trips(u, carry):
        for v in range(unroll):
            t = n_diag + unroll * u + v
            par = (n_diag + v) % 2
            scores(t + 1, 1 - par)
            update(t, par)
        return carry

    lax.fori_loop(0, (n_trips - n_diag - 2) // unroll, trips, 0)
    scores(n_trips - 1, (n_trips - 1) % 2)
    update(n_trips - 2, (n_trips - 2) % 2)
    update(n_trips - 1, (n_trips - 1) % 2)

    lp = lam_ref[...]
    lam = (jnp.exp(jnp.sum(lp[0:1] * lp[1:2], axis=-1, keepdims=True))
           - jnp.exp(jnp.sum(lp[2:3] * lp[3:4], axis=-1, keepdims=True)) + lam_init)
    for c in range(n_chains):
        for own, qb in enumerate((c, nq - 1 - c)):
            inv_l1 = 1.0 / acc_ref[c, own, 0, A_DV:A_DV + 1]
            inv_l2 = lam / acc_ref[c, own, 1, A_DV:A_DV + 1]
            ot = acc_ref[c, own, 0, :A_DV] * inv_l1 - acc_ref[c, own, 1, :A_DV] * inv_l2
            ms = jnp.mean(ot * ot, axis=0, keepdims=True)
            ot = ot * (lax.rsqrt(ms + EPS) * (1.0 - lam_init)) * g_ref[...]
            rows = slice(qb * tq, (qb + 1) * tq)
            o_ref[rows, :] = (ot.T * jax.nn.silu(gate_ref[rows, :].astype(F32))
                              ).astype(o_ref.dtype)


def _diff_attn(p, lam_p, subln_g, *, batch, seq, lam_init, tq, tk):
    t = batch * seq
    nq, nd = seq // tq, tq // tk
    assert nq % 2 == 0
    table = _attn_schedule(nq, nd)
    kernel = functools.partial(_diff_attn_kernel, tq=tq, tk=tk, seq=seq, lam_init=lam_init)
    acc_rows = A_DV + BF16_SUBLANES
    return pl.pallas_call(
        kernel,
        grid=(batch, A_HEADS),
        in_specs=[
            pl.BlockSpec(memory_space=pltpu.SMEM),
            pl.BlockSpec((seq, 128), lambda b, h: (b, COL_QA + h)),
            pl.BlockSpec((seq, 128), lambda b, h: (b, COL_KA + h)),
            pl.BlockSpec((seq, 128), lambda b, h: (b, COL_VA + h)),
            pl.BlockSpec((seq, 128), lambda b, h: (b, COL_GA + h)),
            pl.BlockSpec((4, A_DQK), lambda b, h: (0, 0)),
            pl.BlockSpec((A_DV, 1), lambda b, h: (0, 0)),
        ],
        out_specs=pl.BlockSpec((seq, 128), lambda b, h: (b, h)),
        out_shape=jax.ShapeDtypeStruct((t, BRANCH_WIDTH), BF16),
        scratch_shapes=[
            pltpu.VMEM((acc_rows, seq), BF16),
            pltpu.VMEM((2 * A_DQK, seq), BF16),
            pltpu.VMEM((2 * A_DQK, seq), BF16),
            pltpu.VMEM((nq // 2, 2, tk, tq), F32),
            pltpu.VMEM((nq // 2, 2, tk, tq), F32),
            pltpu.VMEM((nq // 2, 2, 2, 1, tq), F32),
            pltpu.VMEM((nq // 2, 2, 2, acc_rows, tq), F32),
        ],
        compiler_params=pltpu.CompilerParams(
            dimension_semantics=("arbitrary", "arbitrary"),
            vmem_limit_bytes=VMEM_LIMIT_BYTES),
        name="diff_attn",
    )(jnp.asarray(table), p, p, p, p, lam_p, subln_g.reshape(A_DV, 1))


def _block_cumsum(x, rows, blk):
    r = lax.broadcasted_iota(jnp.int32, (rows, rows), 0)
    c = lax.broadcasted_iota(jnp.int32, (rows, rows), 1)
    tri = jnp.where((r // blk == c // blk) & (c <= r), 1.0, 0.0).astype(BF16)
    hi = x.astype(BF16)
    rem = x - hi.astype(F32)
    mid = rem.astype(BF16)
    lo = (rem - mid.astype(F32)).astype(BF16)
    return (jnp.dot(tri, hi, preferred_element_type=F32)
            + jnp.dot(tri, mid, preferred_element_type=F32)
            + jnp.dot(tri, lo, preferred_element_type=F32))


def _hgrn_kernel(f_ref, i_ref, q_ref, gate_ref, lbraw_ref, ng_ref, o_ref,
                 st_ref, b_s, q_s, lf_s, k_s, o_s, *, layer, rows):
    @pl.when(pl.program_id(1) == 0)
    def _():
        st_ref[...] = jnp.zeros(st_ref.shape, F32)

    raw = lbraw_ref[...]
    e = jnp.exp(raw - jnp.max(raw, axis=0, keepdims=True))
    w = e / jnp.sum(e, axis=0, keepdims=True)
    lb = jnp.zeros((1, raw.shape[1]), F32)
    for j in range(1, layer + 1):
        lb = lb + w[j:j + 1]

    z = f_ref[...].astype(F32)
    e = jnp.exp(-jnp.abs(z))
    inv = 1.0 / (1.0 + e)
    log_sig = jnp.minimum(z, 0.0) + jnp.log(inv)
    k_s[...] = (1.0 - lb) * (jnp.where(z >= 0.0, e, 1.0) * inv)
    a = jnp.log(lb)
    c = jnp.log1p(-lb) + log_sig
    logf = jnp.maximum(a, c) + jnp.log(1.0 + jnp.exp(-jnp.abs(a - c)))
    qf = q_ref[...].astype(F32)
    q_s[...] = qf / (1.0 + jnp.exp(-qf))
    lf_s[...] = logf

    b_s[...] = _block_cumsum(logf, rows, CHUNK)
    totals = jnp.concatenate(
        [b_s[c * CHUNK + CHUNK - 1:(c + 1) * CHUNK, :] for c in range(rows // CHUNK)], axis=0)
    fast = jnp.max(-totals) <= MAX_CHUNK_DECAY

    def finish(rs):
        for h in range(B_HEADS):
            cs = slice(h * B_DK, (h + 1) * B_DK)
            o = o_s[rs, cs]
            ms = jnp.mean(o * o, axis=-1, keepdims=True)
            y = o * lax.rsqrt(ms + EPS) * ng_ref[...]
            o_ref[rs, cs] = (y * jax.nn.silu(gate_ref[rs, cs].astype(F32))).astype(o_ref.dtype)

    @pl.when(fast)
    def _():
        t_i = lax.broadcasted_iota(jnp.int32, (CHUNK, CHUNK), 0)
        s_i = lax.broadcasted_iota(jnp.int32, (CHUNK, CHUNK), 1)
        causal = s_i <= t_i
        for c in range(rows // CHUNK):
            rs = slice(c * CHUNK, (c + 1) * CHUNK)
            weights, inter = [], []
            for h in range(B_HEADS):
                cs = slice(h * B_DK, (h + 1) * B_DK)
                bn, qn, kn, vn = b_s[rs, cs], q_s[rs, cs], k_s[rs, cs], i_ref[rs, cs]
                bl = bn[CHUNK - 1:CHUNK]
                half = 0.5 * bl
                qp = (qn * jnp.exp(bn - half)).astype(BF16)
                kp = (kn * jnp.exp(half - bn)).astype(BF16)
                weights.append(_nt_dot(qp, kp))
                st = st_ref[h]
                qe = (qn * jnp.exp(bn)).astype(BF16)
                inter.append(_nt_dot(qe, st.astype(BF16)))
                kd = (kn * jnp.exp(bl - bn)).astype(BF16)
                st_ref[h] = st * jnp.exp(bl) + _tn_dot(vn, kd)
            for h in range(B_HEADS):
                cs = slice(h * B_DK, (h + 1) * B_DK)
                a = jnp.where(causal, weights[h], 0.0).astype(BF16)
                o_s[rs, cs] = inter[h] + jnp.dot(a, i_ref[rs, cs], preferred_element_type=F32)
            finish(rs)

    @pl.when(jnp.logical_not(fast))
    def _():
        b_s[...] = _block_cumsum(lf_s[...], rows, SUB)
        t_idx = lax.broadcasted_iota(jnp.int32, (SUB, 1), 0)

        def body(n, carry):
            r0 = pl.multiple_of(n * SUB, SUB)
            for h in range(B_HEADS):
                cs = slice(h * B_DK, (h + 1) * B_DK)
                bn = b_s[pl.ds(r0, SUB), cs]
                qn = q_s[pl.ds(r0, SUB), cs]
                kn = k_s[pl.ds(r0, SUB), cs]
                vn = i_ref[pl.ds(r0, SUB), cs]
                vf = vn.astype(F32)
                st = st_ref[h]
                acc = _nt_dot((qn * jnp.exp(bn)).astype(BF16), st.astype(BF16))
                for s in range(SUB):
                    arg = jnp.where(t_idx >= s, bn - bn[s:s + 1], -jnp.inf)
                    wts = qn * (kn[s:s + 1] * jnp.exp(arg))
                    acc = acc + jnp.sum(wts, axis=-1, keepdims=True) * vf[s:s + 1]
                o_s[pl.ds(r0, SUB), cs] = acc
                bl = bn[SUB - 1:SUB]
                kd = kn * jnp.exp(bl - bn)
                st_ref[h] = st * jnp.exp(bl) + _tn_dot(vn, kd.astype(BF16))
            return carry

        lax.fori_loop(0, rows // SUB, body, 0)
        finish(slice(0, rows))


def _hgrn(p, lb_raw, norm_g, *, batch, seq, layer, rows):
    t = batch * seq
    nr = seq // rows
    kernel = functools.partial(_hgrn_kernel, layer=layer, rows=rows)
    blk = lambda col: pl.BlockSpec((rows, 1024), lambda b, i: (b * nr + i, col // 8))
    depth = lb_raw.shape[0]
    return pl.pallas_call(
        kernel,
        grid=(batch, nr),
        in_specs=[
            blk(COL_FB), blk(COL_IB), blk(COL_QB), blk(COL_GB),
            pl.BlockSpec((depth, 1024), lambda b, i: (0, 0)),
            pl.BlockSpec((1, B_DK), lambda b, i: (0, 0)),
        ],
        out_specs=pl.BlockSpec((rows, 1024), lambda b, i: (b * nr + i, 0)),
        out_shape=jax.ShapeDtypeStruct((t, BRANCH_WIDTH), BF16),
        scratch_shapes=[
            pltpu.VMEM((B_HEADS, B_DK, B_DK), F32),
            pltpu.VMEM((rows, 1024), F32),
            pltpu.VMEM((rows, 1024), F32),
            pltpu.VMEM((rows, 1024), F32),
            pltpu.VMEM((rows, 1024), F32),
            pltpu.VMEM((rows, 1024), F32),
        ],
        compiler_params=pltpu.CompilerParams(
            dimension_semantics=("arbitrary", "arbitrary"),
            vmem_limit_bytes=VMEM_LIMIT_BYTES),
        name="hgrn2",
    )(p, p, p, p, lb_raw, norm_g.reshape(1, B_DK))


def _xattn_kernel(q_ref, gate_ref, k_ref, v_ref, o_ref):
    heads = [slice(h * C_DH, (h + 1) * C_DH) for h in range(C_HEADS)]
    scores = [_nt_dot(q_ref[:, cs], k_ref[:, cs]) for cs in heads]
    c = (C_DH ** -0.5) * math.log2(math.e)
    for cs, s in zip(heads, scores):
        m = jnp.max(s, axis=-1, keepdims=True)
        pr = jnp.exp2((s - m) * c)
        inv_l = 1.0 / jnp.sum(pr, axis=-1, keepdims=True)
        o = jnp.dot(pr.astype(BF16), v_ref[:, cs], preferred_element_type=F32) * inv_l
        o_ref[:, cs] = (o * jax.nn.silu(gate_ref[:, cs].astype(F32))).astype(o_ref.dtype)


def _xattn(p, kv, *, batch, seq, tq):
    t = batch * seq
    nq = seq // tq
    return pl.pallas_call(
        _xattn_kernel,
        grid=(batch, nq),
        in_specs=[
            pl.BlockSpec((tq, 1024), lambda b, i: (b * nq + i, COL_QC_1024)),
            pl.BlockSpec((tq, 1024), lambda b, i: (b * nq + i, COL_GC_1024)),
            pl.BlockSpec((N_MEM, 1024), lambda b, i: (b, 0)),
            pl.BlockSpec((N_MEM, 1024), lambda b, i: (b, 1)),
        ],
        out_specs=pl.BlockSpec((tq, 1024), lambda b, i: (b * nq + i, 0)),
        out_shape=jax.ShapeDtypeStruct((t, BRANCH_WIDTH), BF16),
        compiler_params=pltpu.CompilerParams(
            dimension_semantics=("arbitrary", "arbitrary"),
            vmem_limit_bytes=VMEM_LIMIT_BYTES),
        name="mem_xattn",
    )(p, p, kv, kv)


def _merge_kernel(oa_ref, ob_ref, oc_ref, g0_ref, g1_ref, g2_ref, x_ref, wb_ref, wo_ref,
                  fg_ref, o_ref, *, final):
    half = D_MODEL // 2
    ys = []
    for cols in (slice(0, half), slice(half, D_MODEL)):
        y = None
        for o_r, g_r, j in ((oa_ref, g0_ref, 0), (ob_ref, g1_ref, 1), (oc_ref, g2_ref, 2)):
            term = jax.nn.sigmoid(g_r[:, cols].astype(F32)) * jnp.dot(
                o_r[...], wb_ref[j, :, cols], preferred_element_type=F32)
            y = term if y is None else y + term
        ys.append(y.astype(BF16))
    xn = x_ref[...] + jnp.dot(ys[0], wo_ref[:half, :], preferred_element_type=F32)
    xn = xn + jnp.dot(ys[1], wo_ref[half:, :], preferred_element_type=F32)
    if final:
        ms = jnp.mean(xn * xn, axis=-1, keepdims=True)
        xn = xn * lax.rsqrt(ms + EPS) * fg_ref[...]
    o_ref[...] = xn


def _merge(oa, ob, oc, p, x, wb, wo, final_g, *, layer, tm, final):
    t = x.shape[0]
    row = lambda i: (i, 0)
    const2 = lambda i: (0, 0)
    kernel = functools.partial(_merge_kernel, final=final)
    return pl.pallas_call(
        kernel,
        grid=(t // tm,),
        in_specs=[
            pl.BlockSpec((tm, BRANCH_WIDTH), row),
            pl.BlockSpec((tm, BRANCH_WIDTH), row),
            pl.BlockSpec((tm, BRANCH_WIDTH), row),
            pl.BlockSpec((tm, D_MODEL), lambda i: (i, COL_GL_2048)),
            pl.BlockSpec((tm, D_MODEL), lambda i: (i, COL_GL_2048 + 1)),
            pl.BlockSpec((tm, D_MODEL), lambda i: (i, COL_GL_2048 + 2)),
            pl.BlockSpec((tm, D_MODEL), row),
            pl.BlockSpec((None, N_BRANCH, BRANCH_WIDTH, D_MODEL), lambda i: (layer, 0, 0, 0),
                         pipeline_mode=pl.Buffered(1)),
            pl.BlockSpec((None, D_MODEL, D_MODEL), lambda i: (layer, 0, 0),
                         pipeline_mode=pl.Buffered(1)),
            pl.BlockSpec((1, D_MODEL), const2),
        ],
        out_specs=pl.BlockSpec((tm, D_MODEL), row),
        out_shape=jax.ShapeDtypeStruct((t, D_MODEL), F32),
        compiler_params=pltpu.CompilerParams(
            dimension_semantics=("arbitrary",),
            vmem_limit_bytes=MERGE_VMEM_LIMIT_BYTES),
        name="merge_out",
    )(oa, ob, oc, p, p, p, x, wb, wo, final_g.reshape(1, D_MODEL))


def kernel(x, mem, norm_g, w_in, diff_lambda, diff_subln_g, hgrn_lb_raw, hgrn_norm_g,
           mem_norm_g, w_kv_mem, w_branch, w_out, final_norm_g):
    batch, seq, d = x.shape
    depth = w_in.shape[0]
    xt = x.reshape(batch * seq, d)
    memt = mem.reshape(batch * N_MEM, d)
    w_br_b = w_branch.astype(BF16)
    w_out_b = w_out.astype(BF16)
    lb_raw = hgrn_lb_raw.astype(F32)
    t = TILES
    for l in range(depth):
        lam_init = 0.8 - 0.6 * math.exp(-0.3 * l)
        p = _norm_proj(xt, norm_g[l], w_in, layer=l, tm=t["proj_rows"],
                       tn=t["proj_cols_f32"], tn_rest=t["proj_cols_bf16"])
        kv = _norm_proj(memt, mem_norm_g[l], w_kv_mem, layer=l, tm=t["proj_rows"],
                        tn=t["proj_cols_f32"])
        oa = _diff_attn(p, diff_lambda[l], diff_subln_g[l], batch=batch, seq=seq,
                        lam_init=lam_init, tq=t["attn_q"], tk=t["attn_kv"])
        ob = _hgrn(p, lb_raw, hgrn_norm_g[l], batch=batch, seq=seq, layer=l,
                   rows=t["hgrn_rows"])
        oc = _xattn(p, kv, batch=batch, seq=seq, tq=t["xattn_rows"])
        xt = _merge(oa, ob, oc, p, xt, w_br_b, w_out_b, final_norm_g,
                    layer=l, tm=t["merge_rows"], final=(l == depth - 1))
    return xt.reshape(batch, seq, d)
```
